```python
import math
import jax, jax.numpy as jnp
from jax import lax
import numpy as np

D_MODEL = 1024
BATCH = 32
SEQ = 256
DEPTH = 2
DEC_BATCH = 8
DEC_SEQ = 2048
PAST_LEN = 512

GRID_W = 64
EPS = 1e-6
QBLK = 128
ROPE_THETA = 10000.0

D_RNN = 512
RNN_BLOCKS = 8
RNN_BLK = D_RNN // RNN_BLOCKS
CONV_W = 4
CONV_PAD_L = 2
CONV_PAD_R = CONV_W - 1 - CONV_PAD_L
RG_C = 8.0

MLA_H = 8
MLA_NOPE = 64
MLA_ROPE = 32
MLA_V = 64
Q_RANK = 384
KV_RANK = 256
MLA_SCALE = (MLA_NOPE + MLA_ROPE) ** -0.5

DIFF_H = 4
DIFF_DK = 64
DIFF_DV = 2 * DIFF_DK
DIFF_SCALE = DIFF_DK ** -0.5

N_BRANCH = 3

IN_SPLITS = (D_RNN, D_RNN,
             Q_RANK, KV_RANK, MLA_ROPE, MLA_H * MLA_V,
             DIFF_H * 2 * DIFF_DK, DIFF_H * 2 * DIFF_DK,
             DIFF_H * DIFF_DV, DIFF_H * DIFF_DV,
             N_BRANCH * D_MODEL)
IN_COLS = sum(IN_SPLITS)

kernel_name = 'hybrid_diffusion_rglru_mla_diffattn_step'


def _rmsnorm(x, g):
    xf = x.astype(jnp.float32)
    y = xf * lax.rsqrt(jnp.mean(xf * xf, axis=-1, keepdims=True) + EPS)
    return (y * g.astype(jnp.float32)).astype(x.dtype)


def _split_cols(z):
    offs = np.cumsum(IN_SPLITS)[:-1].tolist()
    return jnp.split(z, offs, axis=-1)


def _rope_1d(x, pos):
    half = x.shape[-1] // 2
    inv = ROPE_THETA ** (-jnp.arange(half, dtype=jnp.float32) / half)
    ang = pos.astype(jnp.float32)[:, None] * inv[None, :]
    cos = jnp.cos(ang)[:, None, :]
    sin = jnp.sin(ang)[:, None, :]
    xf = x.astype(jnp.float32)
    x1, x2 = xf[..., :half], xf[..., half:]
    return jnp.concatenate([x1 * cos - x2 * sin, x1 * sin + x2 * cos], axis=-1).astype(x.dtype)


def _rope_2d(x, pos_row, pos_col):
    d = x.shape[-1] // 2
    return jnp.concatenate([_rope_1d(x[..., :d], pos_row), _rope_1d(x[..., d:], pos_col)], axis=-1)


def _dwconv(x, w, b):
    s = x.shape[1]
    xp = jnp.pad(x, ((0, 0), (CONV_PAD_L, CONV_PAD_R), (0, 0)))
    y = xp[:, 0:s] * w[0]
    for k in range(1, CONV_W):
        y = y + xp[:, k:k + s] * w[k]
    return y + b


def _lin_scan(a, b, h0, reverse):
    idx = -1 if reverse else 0
    b = b.at[:, idx].add(a[:, idx] * h0)

    def comb(e1, e2):
        a1, b1 = e1
        a2, b2 = e2
        return a1 * a2, a2 * b1 + b2

    _, h = lax.associative_scan(comb, (a, b), reverse=reverse, axis=1)
    return h


def _rglru_dir(xc, w_a, b_a, w_x, b_x, lam, h0, reverse):
    bn, s, _ = xc.shape
    xb = xc.reshape(bn, s, RNN_BLOCKS, RNN_BLK)
    r = jax.nn.sigmoid((jnp.einsum('bsnk,nkj->bsnj', xb, w_a.astype(jnp.float32)).reshape(bn, s, D_RNN)
                        + b_a.astype(jnp.float32)))
    i = jax.nn.sigmoid((jnp.einsum('bsnk,nkj->bsnj', xb, w_x.astype(jnp.float32)).reshape(bn, s, D_RNN)
                        + b_x.astype(jnp.float32)))
    log_a = -RG_C * r * jax.nn.softplus(-lam.astype(jnp.float32))
    a = jnp.exp(log_a)
    b = jnp.sqrt(-jnp.expm1(2.0 * log_a)) * (i * xc)
    return _lin_scan(a, b, h0, reverse)


def _sweep_queries(fn, qs):
    sq = qs[0].shape[1]
    blk = min(QBLK, sq)
    nb = sq // blk
    qb = tuple(jnp.moveaxis(q.reshape(q.shape[0], nb, blk, *q.shape[2:]), 1, 0) for q in qs)
    out = lax.map(lambda t: fn(*t), qb)
    out = jnp.moveaxis(out, 0, 1)
    return out.reshape(out.shape[0], sq, *out.shape[3:])


def _mla_up(ckv_n, w_ukv):
    bn, s, _ = ckv_n.shape
    kv = (ckv_n @ w_ukv).reshape(bn, s, MLA_H, MLA_NOPE + MLA_V)
    return kv[..., :MLA_NOPE], kv[..., MLA_NOPE:]


def _mla_attend(qn, qr, kn, kr, v):
    def blockfn(qn_b, qr_b):
        s = jnp.einsum('bqhd,bkhd->bhqk', qn_b, kn) + jnp.einsum('bqhr,bkr->bhqk', qr_b, kr)
        p = jax.nn.softmax(s.astype(jnp.float32) * MLA_SCALE, axis=-1).astype(v.dtype)
        return jnp.einsum('bhqk,bkhd->bqhd', p, v)
    return _sweep_queries(blockfn, (qn, qr))


def _diff_attend(q, k, v, lam):
    def blockfn(q_b):
        s = jnp.einsum('bqhcd,bkhcd->bchqk', q_b, k).astype(jnp.float32) * DIFF_SCALE
        p = jax.nn.softmax(s, axis=-1)
        pd = p[:, 0] - lam * p[:, 1]
        return jnp.einsum('bhqk,bkhd->bqhd', pd.astype(v.dtype), v)
    return _sweep_queries(blockfn, (q,))


def _layer(l, x, cond, lw, pos, ctx):
    (w_mod, b_mod, g_pre, g_post, w_in, conv_w, conv_b, w_rg_a, b_rg_a, w_rg_x, b_rg_x, rg_lam,
     q_norm, w_uq, kv_norm, w_ukv, lam_q1, lam_k1, lam_q2, lam_k2, diff_norm,
     w_br_rnn, w_br_mla, w_br_diff, w_out) = lw
    bn, s, _ = x.shape
    mod = jax.nn.silu(cond) @ w_mod + b_mod
    if mod.ndim == 2:
        mod = mod[:, None, :]
    shift, scale, gate = jnp.split(mod, 3, axis=-1)
    h = _rmsnorm(x, g_pre) * (1 + scale) + shift
    z = h @ w_in
    rx, rg, cq, ckv, kr, mg, dq, dk, dv, dg, mgate = _split_cols(z)

    xc = _dwconv(rx, conv_w, conv_b).astype(jnp.float32)
    if ctx is None:
        h0f = jnp.zeros((bn, D_RNN), jnp.float32)
        h0b = h0f
    else:
        h0f = ctx[4][:, 0].astype(jnp.float32)
        h0b = ctx[4][:, 1].astype(jnp.float32)
    hf = _rglru_dir(xc, w_rg_a[0], b_rg_a[0], w_rg_x[0], b_rg_x[0], rg_lam[0], h0f, False)
    hb = _rglru_dir(xc, w_rg_a[1], b_rg_a[1], w_rg_x[1], b_rg_x[1], rg_lam[1], h0b, True)
    y_rnn = (hf + hb).astype(x.dtype) * jax.nn.silu(rg)

    q = (_rmsnorm(cq, q_norm) @ w_uq).reshape(bn, s, MLA_H, MLA_NOPE + MLA_ROPE)
    qn, qr = q[..., :MLA_NOPE], q[..., MLA_NOPE:]
    ckv_n = _rmsnorm(ckv, kv_norm)
    kn, vm = _mla_up(ckv_n, w_ukv)
    kr_r = kr
    if pos is not None:
        qr = _rope_2d(qr, pos[0], pos[1])
        kr_r = _rope_2d(kr[:, :, None, :], pos[0], pos[1])[:, :, 0]
    if ctx is not None:
        kn_c, vm_c = _mla_up(ctx[0], w_ukv)
        kn_all = jnp.concatenate([kn_c, kn], axis=1)
        kr_all = jnp.concatenate([ctx[1], kr_r], axis=1)
        vm_all = jnp.concatenate([vm_c, vm], axis=1)
    else:
        kn_all, kr_all, vm_all = kn, kr_r, vm
    y_mla = _mla_attend(qn, qr, kn_all, kr_all, vm_all).reshape(bn, s, MLA_H * MLA_V) * jax.nn.silu(mg)

    qd = dq.reshape(bn, s, DIFF_H * 2, DIFF_DK)
    kd = dk.reshape(bn, s, DIFF_H * 2, DIFF_DK)
    if pos is not None:
        qd = _rope_2d(qd, pos[0], pos[1])
        kd = _rope_2d(kd, pos[0], pos[1])
    qd = qd.reshape(bn, s, DIFF_H, 2, DIFF_DK)
    kd_flat = kd.reshape(bn, s, DIFF_H, 2 * DIFF_DK)
    vd = dv.reshape(bn, s, DIFF_H, DIFF_DV)
    if ctx is not None:
        kd_all = jnp.concatenate([ctx[2], kd_flat], axis=1)
        vd_all = jnp.concatenate([ctx[3], vd], axis=1)
    else:
        kd_all, vd_all = kd_flat, vd
    kd_all = kd_all.reshape(bn, kd_all.shape[1], DIFF_H, 2, DIFF_DK)
    lam_init = 0.8 - 0.6 * math.exp(-0.3 * l)
    lam = (jnp.exp(jnp.sum(lam_q1.astype(jnp.float32) * lam_k1.astype(jnp.float32)))
           - jnp.exp(jnp.sum(lam_q2.astype(jnp.float32) * lam_k2.astype(jnp.float32))) + lam_init)
    od = _diff_attend(qd, kd_all, vd_all, lam)
    od = _rmsnorm(od, diff_norm) * (1.0 - lam_init)
    y_diff = od.reshape(bn, s, DIFF_H * DIFF_DV) * jax.nn.silu(dg)

    g = jax.nn.sigmoid(mgate).reshape(bn, s, N_BRANCH, D_MODEL)
    merged = (g[:, :, 0] * (y_rnn @ w_br_rnn) + g[:, :, 1] * (y_mla @ w_br_mla)
              + g[:, :, 2] * (y_diff @ w_br_diff))
    x_new = x + gate * _rmsnorm(merged @ w_out, g_post)
    if ctx is None:
        st = jnp.stack([hf[:, -1], hb[:, 0]], axis=1).astype(x.dtype)
        return x_new, (ckv_n, kr, kd_flat, vd, st)
    return x_new, None


def setup_inputs(seed: int = 0) -> dict:
    key = jax.random.key(seed)
    ks = iter(jax.random.split(key, 40))

    def nrm(shape, s):
        return jax.random.normal(next(ks), shape, jnp.float32) * s

    def gain(shape):
        return 1.0 + nrm(shape, 0.02)

    u = jax.random.uniform(next(ks), (DEPTH, 2, D_RNN), jnp.float32, minval=0.9, maxval=0.999)
    sa = u ** (1.0 / RG_C)
    rg_lam = jnp.log(sa) - jnp.log1p(-sa)
    return {
        'x_prompt': nrm((BATCH, SEQ, D_MODEL), 1.0),
        'x_sample': nrm((DEC_BATCH, DEC_SEQ, D_MODEL), 1.0),
        'cache_mla_ckv': nrm((DEC_BATCH, DEPTH, PAST_LEN, KV_RANK), 1.0),
        'cache_mla_krope': nrm((DEC_BATCH, DEPTH, PAST_LEN, MLA_ROPE), 1.0),
        'cache_diff_k': nrm((DEC_BATCH, DEPTH, PAST_LEN, DIFF_H, 2 * DIFF_DK), 1.0),
        'cache_diff_v': nrm((DEC_BATCH, DEPTH, PAST_LEN, DIFF_H, DIFF_DV), 1.0),
        'state_rnn': nrm((DEC_BATCH, DEPTH, 2, D_RNN), 0.5),
        'c': nrm((DEC_BATCH, D_MODEL), 1.0),
        'c_ctx': nrm((D_MODEL,), 1.0),
        'w_mod': nrm((DEPTH, D_MODEL, 3 * D_MODEL), 0.5 * D_MODEL ** -0.5),
        'b_mod': nrm((DEPTH, 3 * D_MODEL), 0.01),
        'g_pre': gain((DEPTH, D_MODEL)),
        'g_post': gain((DEPTH, D_MODEL)),
        'w_in': nrm((DEPTH, D_MODEL, IN_COLS), D_MODEL ** -0.5),
        'conv_w': nrm((DEPTH, CONV_W, D_RNN), CONV_W ** -0.5),
        'conv_b': nrm((DEPTH, D_RNN), 0.01),
        'w_rg_a': nrm((DEPTH, 2, RNN_BLOCKS, RNN_BLK, RNN_BLK), RNN_BLK ** -0.5),
        'b_rg_a': nrm((DEPTH, 2, D_RNN), 0.01),
        'w_rg_x': nrm((DEPTH, 2, RNN_BLOCKS, RNN_BLK, RNN_BLK), RNN_BLK ** -0.5),
        'b_rg_x': nrm((DEPTH, 2, D_RNN), 0.01),
        'rg_lam': rg_lam,
        'q_norm': gain((DEPTH, Q_RANK)),
        'w_uq': nrm((DEPTH, Q_RANK, MLA_H * (MLA_NOPE + MLA_ROPE)), Q_RANK ** -0.5),
        'kv_norm': gain((DEPTH, KV_RANK)),
        'w_ukv': nrm((DEPTH, KV_RANK, MLA_H * (MLA_NOPE + MLA_V)), KV_RANK ** -0.5),
        'lam_q1': nrm((DEPTH, DIFF_DK), 0.1),
        'lam_k1': nrm((DEPTH, DIFF_DK), 0.1),
        'lam_q2': nrm((DEPTH, DIFF_DK), 0.1),
        'lam_k2': nrm((DEPTH, DIFF_DK), 0.1),
        'diff_norm': gain((DEPTH, DIFF_DV)),
        'w_br_rnn': nrm((DEPTH, D_RNN, D_MODEL), D_RNN ** -0.5),
        'w_br_mla': nrm((DEPTH, MLA_H * MLA_V, D_MODEL), (MLA_H * MLA_V) ** -0.5),
        'w_br_diff': nrm((DEPTH, DIFF_H * DIFF_DV, D_MODEL), (DIFF_H * DIFF_DV) ** -0.5),
        'w_out': nrm((DEPTH, D_MODEL, D_MODEL), D_MODEL ** -0.5),
    }


def reference(x_prompt, x_sample, cache_mla_ckv, cache_mla_krope, cache_diff_k, cache_diff_v, state_rnn,
              c, c_ctx, w_mod, b_mod, g_pre, g_post, w_in, conv_w, conv_b, w_rg_a, b_rg_a, w_rg_x, b_rg_x,
              rg_lam, q_norm, w_uq, kv_norm, w_ukv, lam_q1, lam_k1, lam_q2, lam_k2, diff_norm,
              w_br_rnn, w_br_mla, w_br_diff, w_out):
    weights = (w_mod, b_mod, g_pre, g_post, w_in, conv_w, conv_b, w_rg_a, b_rg_a, w_rg_x, b_rg_x, rg_lam,
               q_norm, w_uq, kv_norm, w_ukv, lam_q1, lam_k1, lam_q2, lam_k2, diff_norm,
               w_br_rnn, w_br_mla, w_br_diff, w_out)

    xp = x_prompt
    ckv_l, kr_l, dk_l, dv_l, st_l = [], [], [], [], []
    for l in range(DEPTH):
        lw = [w[l] for w in weights]
        xp, (ckv_n, kr, kd, vd, st) = _layer(l, xp, c_ctx, lw, None, None)
        ckv_l.append(ckv_n)
        kr_l.append(kr)
        dk_l.append(kd)
        dv_l.append(vd)
        st_l.append(st)
    y_prompt = xp

    n_lat = x_sample.shape[1]
    rows = n_lat // GRID_W
    pos_row = jnp.repeat(jnp.arange(rows, dtype=jnp.int32), GRID_W)
    pos_col = jnp.tile(jnp.arange(GRID_W, dtype=jnp.int32), rows)
    xs = x_sample
    for l in range(DEPTH):
        lw = [w[l] for w in weights]
        ctx = (cache_mla_ckv[:, l], cache_mla_krope[:, l], cache_diff_k[:, l], cache_diff_v[:, l], state_rnn[:, l])
        xs, _ = _layer(l, xs, c, lw, (pos_row, pos_col), ctx)
    y_sample = xs

    new_mla_ckv = jnp.stack(ckv_l, axis=1)
    new_mla_krope = jnp.stack(kr_l, axis=1)
    new_diff_k = jnp.stack(dk_l, axis=1)
    new_diff_v = jnp.stack(dv_l, axis=1)
    new_state_rnn = jnp.stack(st_l, axis=1)
    return (y_prompt, y_sample, new_mla_ckv, new_mla_krope, new_diff_k, new_diff_v, new_state_rnn)
```

```python
import functools
import math

import jax
import jax.numpy as jnp
from jax import lax
from jax.experimental import pallas as pl
from jax.experimental.pallas import tpu as pltpu

F32 = jnp.float32
BF16 = jnp.bfloat16

D_MODEL = 1024
GRID_W = 64
EPS = 1e-6
ROPE_THETA = 10000.0

D_RNN = 512
RNN_BLOCKS = 8
RNN_BLK = D_RNN // RNN_BLOCKS
CONV_W = 4
RG_C = 8.0

MLA_H = 8
MLA_NOPE = 64
MLA_ROPE = 32
MLA_V = 64
Q_RANK = 384
KV_RANK = 256
MLA_SCALE = (MLA_NOPE + MLA_ROPE) ** -0.5

DIFF_H = 4
DIFF_DK = 64
DIFF_DV = 2 * DIFF_DK
DIFF_SCALE = DIFF_DK ** -0.5

LANES = 128
SUBLANES = 8
HEAD_PAD = LANES
VMEM_LIMIT = 56 * 1024 * 1024

C_MGATE = 0
C_RX = 3072
C_RG = 3584
C_MG = 4096
C_DQ = 4608
C_DK = 5120
C_DV = 5632
C_DG = 6144
C_CKV = 6656
C_CQ = 6912
C_KRP = 7296
Z_COLS = 7424

_O_RX, _O_RG, _O_CQ, _O_CKV, _O_KR, _O_MG, _O_DQ, _O_DK, _O_DV, _O_DG, _O_MGATE, _O_END = (
    0, 512, 1024, 1408, 1664, 1696, 2208, 2720, 3232, 3744, 4256, 7328)


def _sigmoid(x):
    return 1.0 / (1.0 + jnp.exp(-x))


def _silu(x):
    return x * _sigmoid(x)


def _rms(x):
    return x * lax.rsqrt(jnp.mean(x * x, axis=-1, keepdims=True) + EPS)


def _params(*sem):
    return pltpu.CompilerParams(dimension_semantics=sem, vmem_limit_bytes=VMEM_LIMIT)


def _mod_kernel(cond_ref, w_ref, b_ref, o_ref):
    c = cond_ref[...]
    o_ref[0] = jnp.dot(_silu(c).astype(BF16), w_ref[0].astype(BF16), preferred_element_type=F32) + b_ref[0]


def _modulation(cond, w_mod, b_mod):
    depth, d, n = w_mod.shape
    r = cond.shape[0]
    tn = 1024
    return pl.pallas_call(
        _mod_kernel,
        grid=(depth, n // tn),
        in_specs=[pl.BlockSpec((r, d), lambda l, j: (0, 0)),
                  pl.BlockSpec((1, d, tn), lambda l, j: (l, 0, j)),
                  pl.BlockSpec((1, 1, tn), lambda l, j: (l, 0, j))],
        out_specs=pl.BlockSpec((1, r, tn), lambda l, j: (l, 0, j)),
        out_shape=jax.ShapeDtypeStruct((depth, r, n), F32),
        compiler_params=_params("arbitrary", "arbitrary"),
        name="modulation",
    )(cond, w_mod, b_mod.reshape(depth, 1, n))


def _inproj_kernel(x_ref, mod_ref, g_ref, w_ref, z_ref):
    y = _rms(x_ref[...]) * g_ref[...]
    h = y * (1.0 + mod_ref[0, 1:2, :]) + mod_ref[0, 0:1, :]
    z_ref[...] = jnp.dot(h.astype(BF16), w_ref[...], preferred_element_type=F32)


def _inproj(x, mod, g_pre, w_perm, rows_per_mod):
    t, d = x.shape
    tm = 512
    nj = 2
    tn = Z_COLS // nj
    tiles_per_mod = rows_per_mod // tm
    return pl.pallas_call(
        _inproj_kernel,
        grid=(nj, t // tm),
        in_specs=[pl.BlockSpec((tm, d), lambda j, i: (i, 0)),
                  pl.BlockSpec((1, 3, d), lambda j, i: (i // tiles_per_mod, 0, 0)),
                  pl.BlockSpec((1, d), lambda j, i: (0, 0)),
                  pl.BlockSpec((d, tn), lambda j, i: (0, j))],
        out_specs=pl.BlockSpec((tm, tn), lambda j, i: (i, j)),
        out_shape=jax.ShapeDtypeStruct((t, Z_COLS), F32),
        compiler_params=_params("arbitrary", "arbitrary"),
        name="inproj",
    )(x, mod, g_pre.reshape(1, d), w_perm)


def _tile_scan(a, b, reverse):
    rows = a.shape[0]
    r_in_tile = lax.broadcasted_iota(jnp.int32, a.shape, 0) & (SUBLANES - 1)
    for k in (1, 2, 4):
        shift = rows - k if reverse else k
        valid = (r_in_tile < SUBLANES - k) if reverse else (r_in_tile >= k)
        a_s = jnp.where(valid, pltpu.roll(a, shift, 0), 1.0)
        b_s = jnp.where(valid, pltpu.roll(b, shift, 0), 0.0)
        b = a * b_s + b
        a = a * a_s
    return a, b


def _rnn_kernel(rx_ref, rg_ref, h0_ref, cw_ref, cb_ref, wg_ref, bg_ref, lam_ref, y_ref, st_ref,
                xpad, hf, hb, *, seq, rows):
    n_chunks = seq // rows
    n_tiles = rows // SUBLANES
    pad = SUBLANES

    xpad[0:pad, :] = jnp.zeros((pad, D_RNN), F32)
    xpad[seq + pad:seq + 2 * pad, :] = jnp.zeros((pad, D_RNN), F32)

    def copy_chunk(c, carry):
        r0 = pl.multiple_of(c * rows, rows)
        xpad[pl.ds(r0 + pad, rows), :] = rx_ref[pl.ds(r0, rows), :]
        return carry

    lax.fori_loop(0, n_chunks, copy_chunk, 0)

    def chunk_maps(r0, d):
        n = rows + 2 * pad
        xe = xpad[pl.ds(r0, n), :]
        taps = (pltpu.roll(xe, 2, 0), pltpu.roll(xe, 1, 0), xe, pltpu.roll(xe, n - 1, 0))
        xc = cb_ref[...]
        for k in range(CONV_W):
            xc = xc + taps[k][pad:pad + rows] * cw_ref[k:k + 1, :]
        g = jnp.dot(xc.astype(BF16), wg_ref[d], preferred_element_type=F32) + bg_ref[d]
        r = _sigmoid(g[:, :D_RNN])
        i = _sigmoid(g[:, D_RNN:])
        neg_lam = -lam_ref[d]
        softplus = jnp.maximum(neg_lam, 0.0) + jnp.log1p(jnp.exp(-jnp.abs(neg_lam)))
        log_a = -RG_C * r * softplus
        a = jnp.exp(log_a)
        b = jnp.sqrt(-jnp.tanh(log_a) * (1.0 + a * a)) * (i * xc)
        return a, b

    def fwd_chunk(c, carry):
        r0 = pl.multiple_of(c * rows, rows)
        a, b = _tile_scan(*chunk_maps(r0, 0), reverse=False)
        a3 = a.reshape(n_tiles, SUBLANES, D_RNN)
        b3 = b.reshape(n_tiles, SUBLANES, D_RNN)
        for j in range(n_tiles):
            ht = a3[j] * carry + b3[j]
            hf[pl.ds(r0 + j * SUBLANES, SUBLANES), :] = ht
            carry = ht[SUBLANES - 1:SUBLANES, :]
        return carry

    def bwd_chunk(cc, carry):
        r0 = pl.multiple_of((n_chunks - 1 - cc) * rows, rows)
        a, b = _tile_scan(*chunk_maps(r0, 1), reverse=True)
        a3 = a.reshape(n_tiles, SUBLANES, D_RNN)
        b3 = b.reshape(n_tiles, SUBLANES, D_RNN)
        for j in reversed(range(n_tiles)):
            ht = a3[j] * carry + b3[j]
            hb[j * SUBLANES:(j + 1) * SUBLANES, :] = ht
            carry = ht[0:1, :]
        y = (hf[pl.ds(r0, rows), :] + hb[...]) * _silu(rg_ref[pl.ds(r0, rows), :])
        y_ref[pl.ds(r0, rows), :] = y.astype(y_ref.dtype)
        return carry

    st_ref[0, 0:1, :] = lax.fori_loop(0, n_chunks, fwd_chunk, h0_ref[0, 0:1, :])
    st_ref[0, 1:2, :] = lax.fori_loop(0, n_chunks, bwd_chunk, h0_ref[0, 1:2, :])


def _rnn(z, h0, conv_w, conv_b, wg, bg, lam, nb, seq):
    t = z.shape[0]
    rows = 256
    kern = functools.partial(_rnn_kernel, seq=seq, rows=rows)
    const2 = lambda b: (0, 0)
    const3 = lambda b: (0, 0, 0)
    return pl.pallas_call(
        kern,
        grid=(nb,),
        in_specs=[pl.BlockSpec((seq, D_RNN), lambda b: (b, C_RX // D_RNN)),
                  pl.BlockSpec((seq, D_RNN), lambda b: (b, C_RG // D_RNN)),
                  pl.BlockSpec((1, 2, D_RNN), lambda b: (b, 0, 0)),
                  pl.BlockSpec((CONV_W, D_RNN), const2),
                  pl.BlockSpec((1, D_RNN), const2),
                  pl.BlockSpec((2, D_RNN, 2 * D_RNN), const3),
                  pl.BlockSpec((2, 1, 2 * D_RNN), const3),
                  pl.BlockSpec((2, 1, D_RNN), const3)],
        out_specs=[pl.BlockSpec((seq, D_RNN), lambda b: (b, 0)),
                   pl.BlockSpec((1, 2, D_RNN), lambda b: (b, 0, 0))],
        out_shape=[jax.ShapeDtypeStruct((t, D_RNN), BF16),
                   jax.ShapeDtypeStruct((nb, 2, D_RNN), F32)],
        scratch_shapes=[pltpu.VMEM((seq + 2 * SUBLANES, D_RNN), F32),
                        pltpu.VMEM((seq, D_RNN), F32),
                        pltpu.VMEM((rows, D_RNN), F32)],
        compiler_params=_params("arbitrary"),
        name="rglru",
    )(z, z, h0, conv_w, conv_b.reshape(1, D_RNN), wg, bg, lam)


def _rope(x, tab_ref, half):
    n = x.shape[-1]
    return x * tab_ref[0] + pltpu.roll(x, n - half, 1) * tab_ref[1] + pltpu.roll(x, half, 1) * tab_ref[2]


def _kv_up(ckvn, krp, wuk_ref, wuv_ref, k_ref, v_ref):
    cb = ckvn.astype(BF16)
    kn = jnp.dot(cb, wuk_ref[...], preferred_element_type=F32)
    v_ref[...] = jnp.dot(cb, wuv_ref[...], preferred_element_type=F32).astype(BF16)
    for h in range(MLA_H):
        sl = slice(h * HEAD_PAD, (h + 1) * HEAD_PAD)
        k_ref[:, sl] = (kn[:, sl] + krp).astype(BF16)


def _prep_kernel(*refs, rope):
    if rope:
        (cq_ref, ckv_ref, krp_ref, dq_ref, dk_ref, dv_ref, tabm_ref, tabd_ref, qn_ref, wuq_ref, kvn_ref, wuk_ref,
         wuv_ref, q_ref, k_ref, v_ref, ckvn_ref, qd_ref, kd_ref, vd_ref) = refs
    else:
        (cq_ref, ckv_ref, krp_ref, dq_ref, dk_ref, dv_ref, qn_ref, wuq_ref, kvn_ref, wuk_ref,
         wuv_ref, q_ref, k_ref, v_ref, ckvn_ref, qd_ref, kd_ref, vd_ref) = refs
    cqn = _rms(cq_ref[...]) * qn_ref[...]
    q = jnp.dot(cqn.astype(BF16), wuq_ref[...], preferred_element_type=F32) * MLA_SCALE
    ckvn = _rms(ckv_ref[...]) * kvn_ref[...]
    ckvn_ref[...] = ckvn
    krp = krp_ref[...]
    if rope:
        krp = _rope(krp, tabm_ref, MLA_ROPE // 4)
    _kv_up(ckvn, krp, wuk_ref, wuv_ref, k_ref, v_ref)
    for h in range(MLA_H):
        sl = slice(h * HEAD_PAD, (h + 1) * HEAD_PAD)
        qh = q[:, sl]
        if rope:
            qh = _rope(qh, tabm_ref, MLA_ROPE // 4)
        q_ref[:, sl] = qh.astype(BF16)
    for h in range(DIFF_H):
        sl = slice(h * LANES, (h + 1) * LANES)
        qd = dq_ref[:, sl]
        kd = dk_ref[:, sl]
        if rope:
            qd = _rope(qd, tabd_ref, DIFF_DK // 4)
            kd = _rope(kd, tabd_ref, DIFF_DK // 4)
        qd_ref[:, sl] = (qd * DIFF_SCALE).astype(BF16)
        kd_ref[:, sl] = kd.astype(BF16)
    vd_ref[...] = dv_ref[...].astype(BF16)


def _prep(z, tabs, q_norm, wuq, kv_norm, wuk, wuv, seq):
    t = z.shape[0]
    tm = 512 if seq % 512 == 0 else 256
    rope = tabs is not None
    zspec = lambda width, col: pl.BlockSpec((tm, width), lambda i: (i, col // width))
    const = lambda shape: pl.BlockSpec(shape, lambda i: (0,) * len(shape))
    in_specs = [zspec(Q_RANK, C_CQ), zspec(KV_RANK, C_CKV), zspec(LANES, C_KRP),
                zspec(512, C_DQ), zspec(512, C_DK), zspec(512, C_DV)]
    args = [z] * 6
    if rope:
        tab_spec = pl.BlockSpec((3, tm, LANES), lambda i: (0, i % (seq // tm), 0))
        in_specs += [tab_spec, tab_spec]
        args += [tabs[0], tabs[1]]
    in_specs += [const((1, Q_RANK)), const((Q_RANK, MLA_H * HEAD_PAD)), const((1, KV_RANK)),
                 const((KV_RANK, MLA_H * HEAD_PAD)), const((KV_RANK, MLA_H * MLA_V))]
    args += [q_norm.reshape(1, Q_RANK), wuq, kv_norm.reshape(1, KV_RANK), wuk, wuv]
    row = lambda width: pl.BlockSpec((tm, width), lambda i: (i, 0))
    widths = (MLA_H * HEAD_PAD, MLA_H * HEAD_PAD, MLA_H * MLA_V, KV_RANK, 512, 512, 512)
    dtypes = (BF16, BF16, BF16, F32, BF16, BF16, BF16)
    return pl.pallas_call(
        functools.partial(_prep_kernel, rope=rope),
        grid=(t // tm,),
        in_specs=in_specs,
        out_specs=[row(w) for w in widths],
        out_shape=[jax.ShapeDtypeStruct((t, w), dt) for w, dt in zip(widths, dtypes)],
        compiler_params=_params("arbitrary"),
        name="attn_prep",
    )(*args)


def _ctx_kv_kernel(ckvn_ref, krp_ref, wuk_ref, wuv_ref, k_ref, v_ref):
    _kv_up(ckvn_ref[...], krp_ref[...], wuk_ref, wuv_ref, k_ref, v_ref)


def _ctx_kv(ckvn, krp, wuk, wuv):
    t = ckvn.shape[0]
    tm = 512
    const = lambda shape: pl.BlockSpec(shape, lambda i: (0,) * len(shape))
    row = lambda width: pl.BlockSpec((tm, width), lambda i: (i, 0))
    return pl.pallas_call(
        _ctx_kv_kernel,
        grid=(t // tm,),
        in_specs=[row(KV_RANK), row(LANES), const((KV_RANK, MLA_H * HEAD_PAD)), const((KV_RANK, MLA_H * MLA_V))],
        out_specs=[row(MLA_H * HEAD_PAD), row(MLA_H * MLA_V)],
        out_shape=[jax.ShapeDtypeStruct((t, MLA_H * HEAD_PAD), BF16),
                   jax.ShapeDtypeStruct((t, MLA_H * MLA_V), BF16)],
        compiler_params=_params("arbitrary"),
        name="ctx_kv",
    )(ckvn, krp, wuk, wuv)


def _attend(q, k_ref, kcols, v_ref, vcols, n_kv, tk):
    tq = q.shape[0]
    dv = vcols.stop - vcols.start

    def body(j, carry):
        m, l, acc = carry
        r0 = pl.multiple_of(j * tk, tk)
        k = k_ref[pl.ds(r0, tk), kcols]
        v = v_ref[pl.ds(r0, tk), vcols]
        s = lax.dot_general(q, k, (((1,), (1,)), ((), ())), preferred_element_type=F32)
        m_new = jnp.maximum(m, jnp.max(s, axis=-1, keepdims=True))
        alpha = jnp.exp(m - m_new)
        p = jnp.exp(s - m_new)
        l = alpha * l + jnp.sum(p, axis=-1, keepdims=True)
        acc = alpha * acc + jnp.dot(p.astype(BF16), v, preferred_element_type=F32)
        return m_new, l, acc

    init = (jnp.full((tq, 1), -1e30, F32), jnp.zeros((tq, 1), F32), jnp.zeros((tq, dv), F32))
    _, l, acc = lax.fori_loop(0, n_kv, body, init)
    return acc / l


def _mla_attn_kernel(q_ref, k_ref, v_ref, mg_ref, o_ref, *, n_kv, tk):
    vcols = slice(0, 2 * MLA_V)
    o0 = _attend(q_ref[:, 0:HEAD_PAD], k_ref, slice(0, HEAD_PAD), v_ref, vcols, n_kv, tk)
    o1 = _attend(q_ref[:, HEAD_PAD:2 * HEAD_PAD], k_ref, slice(HEAD_PAD, 2 * HEAD_PAD), v_ref, vcols, n_kv, tk)
    lane = lax.broadcasted_iota(jnp.int32, o0.shape, 1)
    o = jnp.where(lane < MLA_V, o0, o1)
    o_ref[...] = (o * _silu(mg_ref[...])).astype(o_ref.dtype)


def _mla_attn(q, k, v, z, nb, seq, sk):
    t = q.shape[0]
    tq = 256
    tk = min(512, sk)
    nq = seq // tq
    kern = functools.partial(_mla_attn_kernel, n_kv=sk // tk, tk=tk)
    return pl.pallas_call(
        kern,
        grid=(nb, MLA_H // 2, nq),
        in_specs=[pl.BlockSpec((tq, 2 * HEAD_PAD), lambda b, h, i: (b * nq + i, h)),
                  pl.BlockSpec((sk, 2 * HEAD_PAD), lambda b, h, i: (b, h)),
                  pl.BlockSpec((sk, 2 * MLA_V), lambda b, h, i: (b, h)),
                  pl.BlockSpec((tq, 2 * MLA_V), lambda b, h, i: (b * nq + i, C_MG // (2 * MLA_V) + h))],
        out_specs=pl.BlockSpec((tq, 2 * MLA_V), lambda b, h, i: (b * nq + i, h)),
        out_shape=jax.ShapeDtypeStruct((t, MLA_H * MLA_V), BF16),
        compiler_params=_params("arbitrary", "arbitrary", "arbitrary"),
        name="mla_attn",
    )(q, k, v, z)


def _diff_attn_kernel(q_ref, k_ref, v_ref, dg_ref, lam_ref, dn_ref, o_ref, *, n_kv, tk, lam_init):
    lp = lam_ref[...]
    lam = (jnp.exp(jnp.sum(lp[0:1] * lp[1:2], axis=-1, keepdims=True))
           - jnp.exp(jnp.sum(lp[2:3] * lp[3:4], axis=-1, keepdims=True)) + lam_init)
    q = q_ref[...].astype(F32)
    lane = lax.broadcasted_iota(jnp.int32, q.shape, 1)
    q1 = jnp.where(lane < DIFF_DK, q, 0.0).astype(BF16)
    q2 = jnp.where(lane < DIFF_DK, 0.0, q).astype(BF16)
    cols = slice(0, LANES)
    o1 = _attend(q1, k_ref, cols, v_ref, cols, n_kv, tk)
    o2 = _attend(q2, k_ref, cols, v_ref, cols, n_kv, tk)
    od = o1 - lam * o2
    od = _rms(od) * dn_ref[...] * (1.0 - lam_init)
    o_ref[...] = (od * _silu(dg_ref[...])).astype(o_ref.dtype)


def _diff_attn(qd, kd, vd, z, lam_p, diff_norm, layer, nb, seq, sk):
    t = qd.shape[0]
    tq = 256
    tk = min(512, sk)
    nq = seq // tq
    lam_init = 0.8 - 0.6 * math.exp(-0.3 * layer)
    kern = functools.partial(_diff_attn_kernel, n_kv=sk // tk, tk=tk, lam_init=lam_init)
    return pl.pallas_call(
        kern,
        grid=(nb, DIFF_H, nq),
        in_specs=[pl.BlockSpec((tq, LANES), lambda b, h, i: (b * nq + i, h)),
                  pl.BlockSpec((sk, LANES), lambda b, h, i: (b, h)),
                  pl.BlockSpec((sk, LANES), lambda b, h, i: (b, h)),
                  pl.BlockSpec((tq, LANES), lambda b, h, i: (b * nq + i, C_DG // LANES + h)),
                  pl.BlockSpec((4, DIFF_DK), lambda b, h, i: (0, 0)),
                  pl.BlockSpec((1, DIFF_DV), lambda b, h, i: (0, 0))],
        out_specs=pl.BlockSpec((tq, LANES), lambda b, h, i: (b * nq + i, h)),
        out_shape=jax.ShapeDtypeStruct((t, DIFF_H * DIFF_DV), BF16),
        compiler_params=_params("arbitrary", "arbitrary", "arbitrary"),
        name="diff_attn",
    )(qd, kd, vd, z, lam_p, diff_norm.reshape(1, DIFF_DV))


def _merge_kernel(yr_ref, ym_ref, yd_ref, mgate_ref, x_ref, mod_ref, wr_ref, wm_ref, wd_ref, wo_ref, gp_ref, o_ref):
    merged = _sigmoid(mgate_ref[:, 0:D_MODEL]) * jnp.dot(yr_ref[...], wr_ref[...], preferred_element_type=F32)
    merged += (_sigmoid(mgate_ref[:, D_MODEL:2 * D_MODEL])
               * jnp.dot(ym_ref[...], wm_ref[...], preferred_element_type=F32))
    merged += (_sigmoid(mgate_ref[:, 2 * D_MODEL:3 * D_MODEL])
               * jnp.dot(yd_ref[...], wd_ref[...], preferred_element_type=F32))
    o = jnp.dot(merged.astype(BF16), wo_ref[...], preferred_element_type=F32)
    o_ref[...] = x_ref[...] + mod_ref[0, 2:3, :] * (_rms(o) * gp_ref[...])


def _merge(y_rnn, y_mla, y_diff, z, x, mod, w_br_rnn, w_br_mla, w_br_diff, w_out, g_post, rows_per_mod):
    t, d = x.shape
    tm = 512
    tiles_per_mod = rows_per_mod // tm
    row = lambda width: pl.BlockSpec((tm, width), lambda i: (i, 0))
    const = lambda shape: pl.BlockSpec(shape, lambda i: (0,) * len(shape))
    return pl.pallas_call(
        _merge_kernel,
        grid=(t // tm,),
        in_specs=[row(D_RNN), row(MLA_H * MLA_V), row(DIFF_H * DIFF_DV),
                  pl.BlockSpec((tm, 3 * d), lambda i: (i, C_MGATE // (3 * d))),
                  row(d),
                  pl.BlockSpec((1, 3, d), lambda i: (i // tiles_per_mod, 0, 0)),
                  const((D_RNN, d)), const((MLA_H * MLA_V, d)), const((DIFF_H * DIFF_DV, d)), const((d, d)),
                  const((1, d))],
        out_specs=row(d),
        out_shape=jax.ShapeDtypeStruct((t, d), F32),
        compiler_params=_params("arbitrary"),
        name="merge_out",
    )(y_rnn, y_mla, y_diff, z, x, mod, w_br_rnn, w_br_mla, w_br_diff, w_out, g_post.reshape(1, d))


def _permute_w_in(w):
    d = w.shape[0]
    zeros = lambda n: jnp.zeros((d, n), w.dtype)
    pieces = [w[:, _O_MGATE:_O_END], w[:, _O_RX:_O_RG], w[:, _O_RG:_O_CQ], w[:, _O_MG:_O_DQ], w[:, _O_DQ:_O_DK],
              w[:, _O_DK:_O_DV], w[:, _O_DV:_O_DG], w[:, _O_DG:_O_MGATE], w[:, _O_CKV:_O_KR], w[:, _O_CQ:_O_CKV],
              zeros(MLA_NOPE), w[:, _O_KR:_O_MG], zeros(HEAD_PAD - MLA_NOPE - MLA_ROPE)]
    return jnp.concatenate(pieces, axis=1).astype(BF16)


def _pad_heads(w, width, offset):
    k = w.shape[0]
    w3 = w.reshape(k, MLA_H, width)
    w3 = jnp.pad(w3, ((0, 0), (0, 0), (offset, HEAD_PAD - width - offset)))
    return w3.reshape(k, MLA_H * HEAD_PAD)


def _gate_weights(w_a, b_a, w_x, b_x):
    def dense(w):
        eye = jnp.eye(RNN_BLOCKS, dtype=w.dtype)
        return jnp.einsum('dnkj,nm->dnkmj', w, eye).reshape(2, D_RNN, D_RNN)
    wg = jnp.concatenate([dense(w_a), dense(w_x)], axis=-1).astype(BF16)
    bg = jnp.concatenate([b_a, b_x], axis=-1).reshape(2, 1, 2 * D_RNN)
    return wg, bg


def _rope_tables(n_pos, groups, dim):
    half = dim // 4
    pos = jnp.arange(n_pos, dtype=jnp.int32)
    inv = ROPE_THETA ** (-jnp.arange(half, dtype=F32) / half)
    cos = jnp.ones((n_pos, LANES), F32)
    s1 = jnp.zeros((n_pos, LANES), F32)
    s2 = jnp.zeros((n_pos, LANES), F32)
    for off in groups:
        for axis, p in enumerate((pos // GRID_W, pos % GRID_W)):
            ang = p.astype(F32)[:, None] * inv[None, :]
            base = off + axis * 2 * half
            cos = cos.at[:, base:base + half].set(jnp.cos(ang)).at[:, base + half:base + 2 * half].set(jnp.cos(ang))
            s1 = s1.at[:, base:base + half].set(-jnp.sin(ang))
            s2 = s2.at[:, base + half:base + 2 * half].set(jnp.sin(ang))
    return jnp.stack([cos, s1, s2])


def _layer(l, x, mod, lw, nb, seq, rows_per_mod, h0, tabs, ctx):
    z = _inproj(x, mod, lw['g_pre'], lw['w_in'], rows_per_mod)
    y_rnn, st = _rnn(z, h0, lw['conv_w'], lw['conv_b'], lw['wg'], lw['bg'], lw['lam'], nb, seq)
    q, k, v, ckvn, qd, kd, vd = _prep(z, tabs, lw['q_norm'], lw['wuq'], lw['kv_norm'], lw['wuk'], lw['wuv'], seq)
    sk = seq
    if ctx is not None:
        ctx_ckv, ctx_krp, ctx_dk, ctx_dv = ctx
        past = ctx_ckv.shape[1]
        sk = past + seq
        kc, vc = _ctx_kv(ctx_ckv.reshape(nb * past, KV_RANK), ctx_krp.reshape(nb * past, LANES), lw['wuk'], lw['wuv'])
        cat = lambda a, b: jnp.concatenate([a.reshape(nb, past, -1), b.reshape(nb, seq, -1)], axis=1).reshape(
            nb * sk, -1)
        k, v, kd, vd = cat(kc, k), cat(vc, v), cat(ctx_dk, kd), cat(ctx_dv, vd)
    y_mla = _mla_attn(q, k, v, z, nb, seq, sk)
    y_diff = _diff_attn(qd, kd, vd, z, lw['lam_p'], lw['diff_norm'], l, nb, seq, sk)
    x_new = _merge(y_rnn, y_mla, y_diff, z, x, mod, lw['w_br_rnn'], lw['w_br_mla'], lw['w_br_diff'], lw['w_out'],
                   lw['g_post'], rows_per_mod)
    return x_new, z, ckvn, st


def kernel(x_prompt, x_sample, cache_mla_ckv, cache_mla_krope, cache_diff_k, cache_diff_v, state_rnn, c, c_ctx, w_mod, b_mod, g_pre, g_post, w_in, conv_w, conv_b, w_rg_a, b_rg_a, w_rg_x, b_rg_x, rg_lam, q_norm, w_uq, kv_norm, w_ukv, lam_q1, lam_k1, lam_q2, lam_k2, diff_norm, w_br_rnn, w_br_mla, w_br_diff, w_out):
    depth = w_in.shape[0]
    nbp, sp, d = x_prompt.shape
    nbs, ss, _ = x_sample.shape
    past = cache_mla_ckv.shape[2]

    n_cond = -(-(1 + nbs) // SUBLANES) * SUBLANES
    cond = jnp.concatenate([c_ctx[None, :], c, jnp.zeros((n_cond - 1 - nbs, d), F32)], axis=0)
    mod = _modulation(cond, w_mod, b_mod).reshape(depth, n_cond, 3, d)

    layers = []
    for l in range(depth):
        ukv = w_ukv[l].reshape(KV_RANK, MLA_H, MLA_NOPE + MLA_V)
        wg, bg = _gate_weights(w_rg_a[l], b_rg_a[l], w_rg_x[l], b_rg_x[l])
        layers.append(dict(
            g_pre=g_pre[l], g_post=g_post[l], w_in=_permute_w_in(w_in[l]), conv_w=conv_w[l], conv_b=conv_b[l],
            wg=wg, bg=bg, lam=rg_lam[l].reshape(2, 1, D_RNN), q_norm=q_norm[l], kv_norm=kv_norm[l],
            wuq=_pad_heads(w_uq[l], MLA_NOPE + MLA_ROPE, 0).astype(BF16),
            wuk=_pad_heads(ukv[:, :, :MLA_NOPE].reshape(KV_RANK, MLA_H * MLA_NOPE), MLA_NOPE, 0).astype(BF16),
            wuv=ukv[:, :, MLA_NOPE:].reshape(KV_RANK, MLA_H * MLA_V).astype(BF16),
            lam_p=jnp.stack([lam_q1[l], lam_k1[l], lam_q2[l], lam_k2[l]]), diff_norm=diff_norm[l],
            w_br_rnn=w_br_rnn[l].astype(BF16), w_br_mla=w_br_mla[l].astype(BF16),
            w_br_diff=w_br_diff[l].astype(BF16), w_out=w_out[l].astype(BF16)))

    xp = x_prompt.reshape(nbp * sp, d)
    h0 = jnp.zeros((nbp, 2, D_RNN), F32)
    ckv_l, kr_l, dk_l, dv_l, st_l = [], [], [], [], []
    for l in range(depth):
        xp, z, ckvn, st = _layer(l, xp, mod[l, 0:1], layers[l], nbp, sp, nbp * sp, h0, None, None)
        ckv_l.append(ckvn.reshape(nbp, sp, KV_RANK))
        kr_l.append(z[:, C_KRP + MLA_NOPE:C_KRP + MLA_NOPE + MLA_ROPE].reshape(nbp, sp, MLA_ROPE))
        dk_l.append(z[:, C_DK:C_DK + 512].reshape(nbp, sp, DIFF_H, 2 * DIFF_DK))
        dv_l.append(z[:, C_DV:C_DV + 512].reshape(nbp, sp, DIFF_H, DIFF_DV))
        st_l.append(st)

    tabs = (_rope_tables(ss, (MLA_NOPE,), MLA_ROPE), _rope_tables(ss, (0, DIFF_DK), DIFF_DK))
    xs = x_sample.reshape(nbs * ss, d)
    for l in range(depth):
        ctx = (cache_mla_ckv[:, l],
               jnp.pad(cache_mla_krope[:, l], ((0, 0), (0, 0), (MLA_NOPE, HEAD_PAD - MLA_NOPE - MLA_ROPE))),
               cache_diff_k[:, l].astype(BF16).reshape(nbs, past, DIFF_H * 2 * DIFF_DK),
               cache_diff_v[:, l].astype(BF16).reshape(nbs, past, DIFF_H * DIFF_DV))
        xs, _, _, _ = _layer(l, xs, mod[l, 1:1 + nbs], layers[l], nbs, ss, ss, state_rnn[:, l], tabs, ctx)

    return (xp.reshape(nbp, sp, d), xs.reshape(nbs, ss, d),
            jnp.stack(ckv_l, axis=1), jnp.stack(kr_l, axis=1), jnp.stack(dk_l, axis=1), jnp.stack(dv_l, axis=1),
            jnp.stack(st_l, axis=1))
```

```python
import functools
import math

import jax
import jax.numpy as jnp
import numpy as np
from jax import lax
from jax.experimental import pallas as pl
from jax.experimental.pallas import tpu as pltpu

F32 = jnp.float32
BF16 = jnp.bfloat16

D_MODEL = 1024
GRID_W = 64
EPS = 1e-6
ROPE_THETA = 10000.0
LOG2E = math.log2(math.e)

D_RNN = 512
RNN_BLOCKS = 8
CONV_W = 4
RG_C = 8.0

MLA_H = 8
MLA_NOPE = 64
MLA_ROPE = 32
MLA_V = 64
Q_RANK = 384
KV_RANK = 256
MLA_SCALE = (MLA_NOPE + MLA_ROPE) ** -0.5

DIFF_H = 4
DIFF_DK = 64
DIFF_DV = 2 * DIFF_DK
DIFF_SCALE = DIFF_DK ** -0.5
DIFF_COLS = DIFF_H * DIFF_DV

LANES = 128
SUBLANES = 8
HEAD_PAD = LANES
TQ = 256
VMEM_LIMIT = 56 * 1024 * 1024

_O_CQ, _O_CKV, _O_KR, _O_MG, _O_MGATE, _O_END = 1024, 1408, 1664, 1696, 4256, 7328
C_RX = 0
C_RG = 512
C_LAT = 1024
C_KRP = 1664
C_MG = 1792
C_DQ = 2304
C_DK = 2816
C_DV = 3328
C_DG = 3840
Z_COLS = 4352
LAT_BLOCK = 1024
QKV_BLOCK = 768


def _sigmoid(x):
    return 0.5 * jnp.tanh(0.5 * x) + 0.5


def _silu(x):
    return x * _sigmoid(x)


def _rms(x):
    return x * lax.rsqrt(jnp.mean(x * x, axis=-1, keepdims=True) + EPS)


def _params(*sem):
    return pltpu.CompilerParams(dimension_semantics=sem, vmem_limit_bytes=VMEM_LIMIT)


def _const_spec(shape):
    return pl.BlockSpec(shape, lambda *_: (0,) * len(shape))


def _layer_spec(shape, layer):
    return pl.BlockSpec((None,) + tuple(shape), lambda *_: (layer,) + (0,) * len(shape))


def _mod_kernel(cond_ref, w_ref, b_ref, o_ref):
    c = cond_ref[...]
    o_ref[0] = jnp.dot(_silu(c).astype(BF16), w_ref[0].astype(BF16), preferred_element_type=F32) + b_ref[0]


def _modulation(cond, w_mod, b_mod):
    depth, d, n = w_mod.shape
    r = cond.shape[0]
    tn = 1024
    return pl.pallas_call(
        _mod_kernel,
        grid=(depth, n // tn),
        in_specs=[pl.BlockSpec((r, d), lambda l, j: (0, 0)),
                  pl.BlockSpec((1, d, tn), lambda l, j: (l, 0, j)),
                  pl.BlockSpec((1, 1, tn), lambda l, j: (l, 0, j))],
        out_specs=pl.BlockSpec((1, r, tn), lambda l, j: (l, 0, j)),
        out_shape=jax.ShapeDtypeStruct((depth, r, n), F32),
        compiler_params=_params("arbitrary", "arbitrary"),
        name="modulation",
    )(cond, w_mod, b_mod.reshape(depth, 1, n))


def _modulated_norm(x, g_ref, mod_ref):
    return (_rms(x) * g_ref[...]) * (1.0 + mod_ref[0, 1:2, :]) + mod_ref[0, 0:1, :]


def _inproj_kernel(x_ref, mod_ref, g_ref, w_ref, z_ref):
    h = _modulated_norm(x_ref[...], g_ref, mod_ref)
    z_ref[...] = jnp.dot(h.astype(BF16), w_ref[...], preferred_element_type=F32)


def _inproj(x, mod, g_pre, w_z, layer, rows_per_mod):
    t, d = x.shape
    tm = 512
    tiles_per_mod = rows_per_mod // tm
    return pl.pallas_call(
        _inproj_kernel,
        grid=(t // tm,),
        in_specs=[pl.BlockSpec((tm, d), lambda i: (i, 0)),
                  pl.BlockSpec((1, 3, d), lambda i: (i // tiles_per_mod, 0, 0)),
                  _layer_spec((1, d), layer),
                  _layer_spec((d, Z_COLS), layer)],
        out_specs=pl.BlockSpec((tm, Z_COLS), lambda i: (i, 0)),
        out_shape=jax.ShapeDtypeStruct((t, Z_COLS), F32),
        compiler_params=_params("arbitrary"),
        name="inproj",
    )(x, mod, g_pre, w_z)


def _tile_scan(a, b, reverse):
    r_in_tile = lax.broadcasted_iota(jnp.int32, a.shape, 1)
    for k in (1, 2, 4):
        shift = SUBLANES - k if reverse else k
        valid = (r_in_tile < SUBLANES - k) if reverse else (r_in_tile >= k)
        a_s = jnp.where(valid, pltpu.roll(a, shift, 1), 1.0)
        b_s = jnp.where(valid, pltpu.roll(b, shift, 1), 0.0)
        b = a * b_s + b
        a = a * a_s
    return a, b


def _rnn_kernel(rx_ref, rg_ref, h0_ref, cw_ref, cb_ref, wg_ref, bg_ref, lam_ref, y_ref, st_ref,
                xpad, hf, hb, *, seq, rows):
    n_chunks = seq // rows
    n_tiles = rows // SUBLANES
    pad = SUBLANES

    xpad[0:pad, :] = jnp.zeros((pad, D_RNN), F32)
    xpad[seq + pad:seq + 2 * pad, :] = jnp.zeros((pad, D_RNN), F32)

    def copy_chunk(c, carry):
        r0 = pl.multiple_of(c * rows, rows)
        xpad[pl.ds(r0 + pad, rows), :] = rx_ref[pl.ds(r0, rows), :]
        return carry

    lax.fori_loop(0, n_chunks, copy_chunk, 0)

    def chunk_maps(r0, d):
        n = rows + 2 * pad
        xe = xpad[pl.ds(r0, n), :]
        taps = (pltpu.roll(xe, 2, 0), pltpu.roll(xe, 1, 0), xe, pltpu.roll(xe, n - 1, 0))
        xc = cb_ref[...]
        for k in range(CONV_W):
            xc = xc + taps[k][pad:pad + rows] * cw_ref[k:k + 1, :]
        g = jnp.dot(xc.astype(BF16), wg_ref[d], preferred_element_type=F32) + bg_ref[d]
        r = _sigmoid(g[:, :D_RNN])
        i = _sigmoid(g[:, D_RNN:])
        neg_lam = -lam_ref[d]
        softplus = jnp.maximum(neg_lam, 0.0) + jnp.log1p(jnp.exp(-jnp.abs(neg_lam)))
        log_a = -RG_C * r * softplus
        a = jnp.exp(log_a)
        b = jnp.sqrt(-jnp.tanh(log_a) * (1.0 + a * a)) * (i * xc)
        return a.reshape(n_tiles, SUBLANES, D_RNN), b.reshape(n_tiles, SUBLANES, D_RNN)

    def fwd_chunk(c, carry):
        r0 = pl.multiple_of(c * rows, rows)
        a3, b3 = _tile_scan(*chunk_maps(r0, 0), reverse=False)
        for j in range(n_tiles):
            ht = a3[j] * carry + b3[j]
            hf[pl.ds(r0 + j * SUBLANES, SUBLANES), :] = ht
            carry = ht[SUBLANES - 1:SUBLANES, :]
        return carry

    def bwd_chunk(cc, carry):
        r0 = pl.multiple_of((n_chunks - 1 - cc) * rows, rows)
        a3, b3 = _tile_scan(*chunk_maps(r0, 1), reverse=True)
        for j in reversed(range(n_tiles)):
            ht = a3[j] * carry + b3[j]
            hb[j * SUBLANES:(j + 1) * SUBLANES, :] = ht
            carry = ht[0:1, :]
        y = (hf[pl.ds(r0, rows), :] + hb[...]) * _silu(rg_ref[pl.ds(r0, rows), :])
        y_ref[pl.ds(r0, rows), :] = y.astype(y_ref.dtype)
        return carry

    st_ref[0, 0:1, :] = lax.fori_loop(0, n_chunks, fwd_chunk, h0_ref[0, 0:1, :])
    st_ref[0, 1:2, :] = lax.fori_loop(0, n_chunks, bwd_chunk, h0_ref[0, 1:2, :])


def _rnn(z, h0, lw, layer, nb, seq):
    t = z.shape[0]
    rows = 256
    kern = functools.partial(_rnn_kernel, seq=seq, rows=rows)
    return pl.pallas_call(
        kern,
        grid=(nb,),
        in_specs=[pl.BlockSpec((seq, D_RNN), lambda b: (b, C_RX // D_RNN)),
                  pl.BlockSpec((seq, D_RNN), lambda b: (b, C_RG // D_RNN)),
                  pl.BlockSpec((1, 2, D_RNN), lambda b: (b, 0, 0)),
                  _layer_spec((CONV_W, D_RNN), layer),
                  _layer_spec((1, D_RNN), layer),
                  _layer_spec((2, D_RNN, 2 * D_RNN), layer),
                  _layer_spec((2, 1, 2 * D_RNN), layer),
                  _layer_spec((2, 1, D_RNN), layer)],
        out_specs=[pl.BlockSpec((seq, D_RNN), lambda b: (b, 0)),
                   pl.BlockSpec((1, 2, D_RNN), lambda b: (b, 0, 0))],
        out_shape=[jax.ShapeDtypeStruct((t, D_RNN), BF16),
                   jax.ShapeDtypeStruct((nb, 2, D_RNN), F32)],
        scratch_shapes=[pltpu.VMEM((seq + 2 * SUBLANES, D_RNN), F32),
                        pltpu.VMEM((seq, D_RNN), F32),
                        pltpu.VMEM((rows, D_RNN), F32)],
        compiler_params=_params("arbitrary"),
        name="rglru",
    )(z, z, h0, lw['conv_w'], lw['conv_b'], lw['wg'], lw['bg'], lw['lam'])


def _rope(x, tab_ref, half):
    n = x.shape[-1]
    return x * tab_ref[0] + pltpu.roll(x, n - half, 1) * tab_ref[1] + pltpu.roll(x, half, 1) * tab_ref[2]


def _nt_dot(a, b):
    return lax.dot_general(a, b, (((1,), (1,)), ((), ())), preferred_element_type=F32)


def _kv_up(ckvn, krp, wuk_ref, wuvt_ref, k_ref, vt_ref):
    cb = ckvn.astype(BF16)
    kn = jnp.dot(cb, wuk_ref[...], preferred_element_type=F32)
    vt_ref[0] = _nt_dot(wuvt_ref[...], cb).astype(BF16)
    for h in range(MLA_H):
        sl = slice(h * HEAD_PAD, (h + 1) * HEAD_PAD)
        k_ref[:, sl] = (kn[:, sl] + krp).astype(BF16)


def _prep_kernel(*refs, rope):
    if rope:
        (lat_ref, qa_ref, qb_ref, tabm_ref, tabd_ref, qn_ref, wuq_ref, kvn_ref, wuk_ref, wuvt_ref,
         q_ref, k_ref, vt_ref, ckvn_ref, qd_ref, kd_ref, vdt_ref) = refs
    else:
        (lat_ref, qa_ref, qb_ref, qn_ref, wuq_ref, kvn_ref, wuk_ref, wuvt_ref,
         q_ref, k_ref, vt_ref, ckvn_ref, qd_ref, kd_ref, vdt_ref) = refs
        tabm_ref = tabd_ref = None
    cqn = _rms(lat_ref[:, 0:Q_RANK]) * qn_ref[...]
    q = jnp.dot(cqn.astype(BF16), wuq_ref[...], preferred_element_type=F32) * (MLA_SCALE * LOG2E)
    ckvn = _rms(lat_ref[:, Q_RANK:Q_RANK + KV_RANK]) * kvn_ref[...]
    ckvn_ref[...] = ckvn
    krp = lat_ref[:, C_KRP - C_LAT:C_KRP - C_LAT + LANES]
    if rope:
        krp = _rope(krp, tabm_ref, MLA_ROPE // 4)
    _kv_up(ckvn, krp, wuk_ref, wuvt_ref, k_ref, vt_ref)
    for h in range(MLA_H):
        sl = slice(h * HEAD_PAD, (h + 1) * HEAD_PAD)
        qh = q[:, sl]
        if rope:
            qh = _rope(qh, tabm_ref, MLA_ROPE // 4)
        q_ref[:, sl] = qh.astype(BF16)

    def piece(col):
        ref = qa_ref if col < QKV_BLOCK else qb_ref
        return ref[:, col % QKV_BLOCK:col % QKV_BLOCK + LANES]

    for h in range(DIFF_H):
        sl = slice(h * LANES, (h + 1) * LANES)
        qd = piece(h * LANES)
        kd = piece(DIFF_COLS + h * LANES)
        if rope:
            qd = _rope(qd, tabd_ref, DIFF_DK // 4)
            kd = _rope(kd, tabd_ref, DIFF_DK // 4)
        qd_ref[:, sl] = (qd * (DIFF_SCALE * LOG2E)).astype(BF16)
        kd_ref[:, sl] = kd.astype(BF16)
        vdt_ref[0, sl, :] = piece(2 * DIFF_COLS + h * LANES).T.astype(BF16)


def _prep(z, tabs, lw, layer, seq, tk):
    t = z.shape[0]
    tm = tk
    rope = tabs is not None
    in_specs = [pl.BlockSpec((tm, LAT_BLOCK), lambda i: (i, C_LAT // LAT_BLOCK)),
                pl.BlockSpec((tm, QKV_BLOCK), lambda i: (i, C_DQ // QKV_BLOCK)),
                pl.BlockSpec((tm, QKV_BLOCK), lambda i: (i, C_DQ // QKV_BLOCK + 1))]
    args = [z, z, z]
    if rope:
        tab_spec = pl.BlockSpec((3, tm, LANES), lambda i: (0, i % (seq // tm), 0))
        in_specs += [tab_spec, tab_spec]
        args += [tabs[0], tabs[1]]
    in_specs += [_layer_spec((1, Q_RANK), layer), _layer_spec((Q_RANK, MLA_H * HEAD_PAD), layer),
                 _layer_spec((1, KV_RANK), layer), _layer_spec((KV_RANK, MLA_H * HEAD_PAD), layer),
                 _layer_spec((MLA_H * MLA_V, KV_RANK), layer)]
    args += [lw['q_norm'], lw['wuq'], lw['kv_norm'], lw['wuk'], lw['wuvt']]
    row = lambda width: pl.BlockSpec((tm, width), lambda i: (i, 0))
    tr = lambda rows: pl.BlockSpec((1, rows, tm), lambda i: (i, 0, 0))
    qk = MLA_H * HEAD_PAD
    return pl.pallas_call(
        functools.partial(_prep_kernel, rope=rope),
        grid=(t // tm,),
        in_specs=in_specs,
        out_specs=[row(qk), row(qk), tr(MLA_H * MLA_V), row(KV_RANK), row(DIFF_COLS), row(DIFF_COLS), tr(DIFF_COLS)],
        out_shape=[jax.ShapeDtypeStruct((t, qk), BF16), jax.ShapeDtypeStruct((t, qk), BF16),
                   jax.ShapeDtypeStruct((t // tm, MLA_H * MLA_V, tm), BF16),
                   jax.ShapeDtypeStruct((t, KV_RANK), F32),
                   jax.ShapeDtypeStruct((t, DIFF_COLS), BF16), jax.ShapeDtypeStruct((t, DIFF_COLS), BF16),
                   jax.ShapeDtypeStruct((t // tm, DIFF_COLS, tm), BF16)],
        compiler_params=_params("arbitrary"),
        name="attn_prep",
    )(*args)


def _ctx_kv_kernel(ckvn_ref, krp_ref, wuk_ref, wuvt_ref, k_ref, vt_ref):
    _kv_up(ckvn_ref[...], krp_ref[...], wuk_ref, wuvt_ref, k_ref, vt_ref)


def _ctx_kv(ckvn, krp, lw, layer, tk):
    t = ckvn.shape[0]
    tm = tk
    row = lambda width: pl.BlockSpec((tm, width), lambda i: (i, 0))
    return pl.pallas_call(
        _ctx_kv_kernel,
        grid=(t // tm,),
        in_specs=[row(KV_RANK), row(LANES), _layer_spec((KV_RANK, MLA_H * HEAD_PAD), layer),
                  _layer_spec((MLA_H * MLA_V, KV_RANK), layer)],
        out_specs=[row(MLA_H * HEAD_PAD), pl.BlockSpec((1, MLA_H * MLA_V, tm), lambda i: (i, 0, 0))],
        out_shape=[jax.ShapeDtypeStruct((t, MLA_H * HEAD_PAD), BF16),
                   jax.ShapeDtypeStruct((t // tm, MLA_H * MLA_V, tm), BF16)],
        compiler_params=_params("arbitrary"),
        name="ctx_kv",
    )(ckvn, krp, lw['wuk'], lw['wuvt'])


def _attend_t(chains, k_ref, vt_ref, n_kv, tk):
    def scores(j):
        k_blocks = {}
        out = []
        for q, kcols, _ in chains:
            key = (kcols.start, kcols.stop)
            if key not in k_blocks:
                k_blocks[key] = k_ref[j * tk:(j + 1) * tk, kcols]
            out.append(_nt_dot(k_blocks[key], q))
        return out

    state = [None] * len(chains)
    s_cur = scores(0)
    for j in range(n_kv):
        s_next = scores(j + 1) if j + 1 < n_kv else None
        for c, (_, _, vrows) in enumerate(chains):
            st = s_cur[c]
            m_blk = jnp.max(st, axis=0, keepdims=True)
            if state[c] is None:
                m_new = m_blk
            else:
                m, l, acc = state[c]
                m_new = jnp.maximum(m, m_blk)
                alpha = jnp.exp2(m - m_new)
            p = jnp.exp2(st - m_new)
            l_blk = jnp.sum(p, axis=0, keepdims=True)
            pv = jnp.dot(vt_ref[j, vrows, :], p.astype(BF16), preferred_element_type=F32)
            state[c] = (m_new, l_blk, pv) if state[c] is None else (m_new, alpha * l + l_blk, alpha * acc + pv)
        s_cur = s_next
    return [acc / l for _, l, acc in state]


def _mla_attn_kernel(q_ref, k_ref, vt_ref, mg_ref, o_ref, *, n_kv, tk, n_sub):
    chains = [(q_ref[s * TQ:(s + 1) * TQ, h * HEAD_PAD:(h + 1) * HEAD_PAD],
               slice(h * HEAD_PAD, (h + 1) * HEAD_PAD), slice(h * MLA_V, (h + 1) * MLA_V))
              for s in range(n_sub) for h in range(2)]
    outs = _attend_t(chains, k_ref, vt_ref, n_kv, tk)
    for s in range(n_sub):
        rows = slice(s * TQ, (s + 1) * TQ)
        ot = jnp.concatenate(outs[2 * s:2 * s + 2], axis=0)
        o_ref[rows, :] = (ot.T * _silu(mg_ref[rows, :])).astype(o_ref.dtype)


def _mla_attn(q, k, vt, z, nb, seq, sk, tk):
    t = q.shape[0]
    n_sub = min(2, seq // TQ)
    tq = n_sub * TQ
    nq = seq // tq
    n_kv = sk // tk
    kern = functools.partial(_mla_attn_kernel, n_kv=n_kv, tk=tk, n_sub=n_sub)
    return pl.pallas_call(
        kern,
        grid=(nb, MLA_H // 2, nq),
        in_specs=[pl.BlockSpec((tq, 2 * HEAD_PAD), lambda b, h, i: (b * nq + i, h)),
                  pl.BlockSpec((sk, 2 * HEAD_PAD), lambda b, h, i: (b, h)),
                  pl.BlockSpec((n_kv, 2 * MLA_V, tk), lambda b, h, i: (b, h, 0)),
                  pl.BlockSpec((tq, 2 * MLA_V), lambda b, h, i: (b * nq + i, C_MG // (2 * MLA_V) + h))],
        out_specs=pl.BlockSpec((tq, 2 * MLA_V), lambda b, h, i: (b * nq + i, h)),
        out_shape=jax.ShapeDtypeStruct((t, MLA_H * MLA_V), BF16),
        compiler_params=_params("arbitrary", "arbitrary", "arbitrary"),
        name="mla_attn",
    )(q, k, vt, z)


def _diff_attn_kernel(q_ref, k_ref, vt_ref, dg_ref, lam_ref, dn_ref, o_ref, *, n_kv, tk, n_sub, lam_init):
    lp = lam_ref[...]
    lam = (jnp.exp(jnp.sum(lp[0:1] * lp[1:2], axis=-1, keepdims=True))
           - jnp.exp(jnp.sum(lp[2:3] * lp[3:4], axis=-1, keepdims=True)) + lam_init)
    full = slice(0, LANES)
    chains = []
    for s in range(n_sub):
        q = q_ref[s * TQ:(s + 1) * TQ, :].astype(F32)
        lane = lax.broadcasted_iota(jnp.int32, q.shape, 1)
        chains.append((jnp.where(lane < DIFF_DK, q, 0.0).astype(BF16), full, full))
        chains.append((jnp.where(lane < DIFF_DK, 0.0, q).astype(BF16), full, full))
    outs = _attend_t(chains, k_ref, vt_ref, n_kv, tk)
    for s in range(n_sub):
        rows = slice(s * TQ, (s + 1) * TQ)
        od = (outs[2 * s] - lam * outs[2 * s + 1]).T
        od = _rms(od) * dn_ref[...] * (1.0 - lam_init)
        o_ref[rows, :] = (od * _silu(dg_ref[rows, :])).astype(o_ref.dtype)


def _diff_attn(qd, kd, vdt, z, lw, layer, nb, seq, sk, tk):
    t = qd.shape[0]
    n_sub = min(2, seq // TQ)
    tq = n_sub * TQ
    nq = seq // tq
    n_kv = sk // tk
    lam_init = 0.8 - 0.6 * math.exp(-0.3 * layer)
    kern = functools.partial(_diff_attn_kernel, n_kv=n_kv, tk=tk, n_sub=n_sub, lam_init=lam_init)
    return pl.pallas_call(
        kern,
        grid=(nb, DIFF_H, nq),
        in_specs=[pl.BlockSpec((tq, LANES), lambda b, h, i: (b * nq + i, h)),
                  pl.BlockSpec((sk, LANES), lambda b, h, i: (b, h)),
                  pl.BlockSpec((n_kv, LANES, tk), lambda b, h, i: (b, h, 0)),
                  pl.BlockSpec((tq, LANES), lambda b, h, i: (b * nq + i, C_DG // LANES + h)),
                  _layer_spec((4, DIFF_DK), layer),
                  _layer_spec((1, DIFF_DV), layer)],
        out_specs=pl.BlockSpec((tq, LANES), lambda b, h, i: (b * nq + i, h)),
        out_shape=jax.ShapeDtypeStruct((t, DIFF_COLS), BF16),
        compiler_params=_params("arbitrary", "arbitrary", "arbitrary"),
        name="diff_attn",
    )(qd, kd, vdt, z, lw['lam_p'], lw['diff_norm'])


def _merge_kernel(yr_ref, ym_ref, yd_ref, x_ref, mod_ref, gpre_ref, wg_ref, wr_ref, wm_ref, wd_ref, wo_ref,
                  gpost_ref, o_ref):
    x = x_ref[...]
    h = _modulated_norm(x, gpre_ref, mod_ref).astype(BF16)
    merged = None
    for k, (y_ref, w_ref) in enumerate(((yr_ref, wr_ref), (ym_ref, wm_ref), (yd_ref, wd_ref))):
        gate = _sigmoid(jnp.dot(h, wg_ref[:, k * D_MODEL:(k + 1) * D_MODEL], preferred_element_type=F32))
        term = gate * jnp.dot(y_ref[...], w_ref[...], preferred_element_type=F32)
        merged = term if merged is None else merged + term
    o = jnp.dot(merged.astype(BF16), wo_ref[...], preferred_element_type=F32)
    o_ref[...] = x + mod_ref[0, 2:3, :] * (_rms(o) * gpost_ref[...])


def _merge(y_rnn, y_mla, y_diff, x, mod, lw, layer, rows_per_mod):
    t, d = x.shape
    tm = 512
    tiles_per_mod = rows_per_mod // tm
    row = lambda width: pl.BlockSpec((tm, width), lambda i: (i, 0))
    return pl.pallas_call(
        _merge_kernel,
        grid=(t // tm,),
        in_specs=[row(D_RNN), row(MLA_H * MLA_V), row(DIFF_COLS), row(d),
                  pl.BlockSpec((1, 3, d), lambda i: (i // tiles_per_mod, 0, 0)),
                  _layer_spec((1, d), layer), _layer_spec((d, 3 * d), layer),
                  _layer_spec((D_RNN, d), layer), _layer_spec((MLA_H * MLA_V, d), layer),
                  _layer_spec((DIFF_COLS, d), layer), _layer_spec((d, d), layer), _layer_spec((1, d), layer)],
        out_specs=row(d),
        out_shape=jax.ShapeDtypeStruct((t, d), F32),
        compiler_params=_params("arbitrary"),
        name="merge_out",
    )(y_rnn, y_mla, y_diff, x, mod, lw['g_pre'], lw['w_mgate'], lw['w_br_rnn'], lw['w_br_mla'], lw['w_br_diff'],
      lw['w_out'], lw['g_post'])


def _pad_heads(w, width):
    l, k, _ = w.shape
    w4 = jnp.pad(w.reshape(l, k, MLA_H, width), ((0, 0), (0, 0), (0, 0), (0, HEAD_PAD - width)))
    return w4.reshape(l, k, MLA_H * HEAD_PAD)


def _layer_params(p):
    depth, d, _ = p['w_in'].shape
    w_in = p['w_in']
    zeros = lambda n: jnp.zeros((depth, d, n), w_in.dtype)
    w_z = jnp.concatenate([w_in[:, :, :_O_KR], zeros(MLA_NOPE), w_in[:, :, _O_KR:_O_MG],
                           zeros(HEAD_PAD - MLA_NOPE - MLA_ROPE), w_in[:, :, _O_MG:_O_MGATE]], axis=2).astype(BF16)
    eye = jnp.eye(RNN_BLOCKS, dtype=F32)
    dense = lambda w: jnp.einsum('ldnkj,nm->ldnkmj', w, eye).reshape(depth, 2, D_RNN, D_RNN)
    ukv = p['w_ukv'].reshape(depth, KV_RANK, MLA_H, MLA_NOPE + MLA_V)
    return dict(
        g_pre=p['g_pre'].reshape(depth, 1, d), g_post=p['g_post'].reshape(depth, 1, d),
        w_z=w_z, w_mgate=w_in[:, :, _O_MGATE:_O_END].astype(BF16),
        conv_w=p['conv_w'], conv_b=p['conv_b'].reshape(depth, 1, D_RNN),
        wg=jnp.concatenate([dense(p['w_rg_a']), dense(p['w_rg_x'])], axis=-1).astype(BF16),
        bg=jnp.concatenate([p['b_rg_a'], p['b_rg_x']], axis=-1).reshape(depth, 2, 1, 2 * D_RNN),
        lam=p['rg_lam'].reshape(depth, 2, 1, D_RNN),
        q_norm=p['q_norm'].reshape(depth, 1, Q_RANK), kv_norm=p['kv_norm'].reshape(depth, 1, KV_RANK),
        wuq=_pad_heads(p['w_uq'], MLA_NOPE + MLA_ROPE).astype(BF16),
        wuk=_pad_heads(ukv[..., :MLA_NOPE].reshape(depth, KV_RANK, MLA_H * MLA_NOPE), MLA_NOPE).astype(BF16),
        wuvt=jnp.swapaxes(ukv[..., MLA_NOPE:].reshape(depth, KV_RANK, MLA_H * MLA_V), 1, 2).astype(BF16),
        lam_p=jnp.stack([p['lam_q1'], p['lam_k1'], p['lam_q2'], p['lam_k2']], axis=1),
        diff_norm=p['diff_norm'].reshape(depth, 1, DIFF_DV),
        w_br_rnn=p['w_br_rnn'].astype(BF16), w_br_mla=p['w_br_mla'].astype(BF16),
        w_br_diff=p['w_br_diff'].astype(BF16), w_out=p['w_out'].astype(BF16))


def _rope_tables(n_pos, groups, dim):
    half = dim // 4
    inv = ROPE_THETA ** (-np.arange(half, dtype=np.float32) / half)
    inv_row = np.zeros((LANES,), np.float32)
    inv_col = np.zeros((LANES,), np.float32)
    first = np.zeros((LANES,), np.float32)
    second = np.zeros((LANES,), np.float32)
    for off in groups:
        for inv_axis, base in ((inv_row, off), (inv_col, off + 2 * half)):
            inv_axis[base:base + half] = inv
            inv_axis[base + half:base + 2 * half] = inv
            first[base:base + half] = 1.0
            second[base + half:base + 2 * half] = 1.0
    pos = jnp.arange(n_pos, dtype=jnp.int32)
    ang = ((pos // GRID_W).astype(F32)[:, None] * jnp.asarray(inv_row)[None, :]
           + (pos % GRID_W).astype(F32)[:, None] * jnp.asarray(inv_col)[None, :])
    sin = jnp.sin(ang)
    return jnp.stack([jnp.cos(ang), -sin * jnp.asarray(first), sin * jnp.asarray(second)])


def _layer(l, x, mod, lw, nb, seq, rows_per_mod, h0, tabs, ctx):
    tk = min(512, seq)
    z = _inproj(x, mod, lw['g_pre'], lw['w_z'], l, rows_per_mod)
    y_rnn, st = _rnn(z, h0, lw, l, nb, seq)
    q, k, vt, ckvn, qd, kd, vdt = _prep(z, tabs, lw, l, seq, tk)
    sk = seq
    if ctx is not None:
        ctx_ckv, ctx_krp, ctx_dk, ctx_dvt = ctx
        past = ctx_ckv.shape[1]
        sk = past + seq
        kc, vtc = _ctx_kv(ctx_ckv.reshape(nb * past, KV_RANK), ctx_krp.reshape(nb * past, LANES), lw, l, tk)
        cat = lambda a, b: jnp.concatenate([a.reshape((nb, -1) + a.shape[1:]), b.reshape((nb, -1) + b.shape[1:])],
                                           axis=1).reshape((-1,) + a.shape[1:])
        k, vt, kd, vdt = cat(kc, k), cat(vtc, vt), cat(ctx_dk, kd), cat(ctx_dvt, vdt)
    y_mla = _mla_attn(q, k, vt, z, nb, seq, sk, tk)
    y_diff = _diff_attn(qd, kd, vdt, z, lw, l, nb, seq, sk, tk)
    x_new = _merge(y_rnn, y_mla, y_diff, x, mod, lw, l, rows_per_mod)
    return x_new, z, ckvn, st


def kernel(x_prompt, x_sample, cache_mla_ckv, cache_mla_krope, cache_diff_k, cache_diff_v, state_rnn, c, c_ctx, w_mod, b_mod, g_pre, g_post, w_in, conv_w, conv_b, w_rg_a, b_rg_a, w_rg_x, b_rg_x, rg_lam, q_norm, w_uq, kv_norm, w_ukv, lam_q1, lam_k1, lam_q2, lam_k2, diff_norm, w_br_rnn, w_br_mla, w_br_diff, w_out):
    depth = w_in.shape[0]
    nbp, sp, d = x_prompt.shape
    nbs, ss, _ = x_sample.shape
    past = cache_mla_ckv.shape[2]

    n_cond = -(-(1 + nbs) // SUBLANES) * SUBLANES
    cond = jnp.concatenate([c_ctx[None, :], c, jnp.zeros((n_cond - 1 - nbs, d), F32)], axis=0)
    mod = _modulation(cond, w_mod, b_mod).reshape(depth, n_cond, 3, d)

    lw = _layer_params(dict(
        w_in=w_in, g_pre=g_pre, g_post=g_post, conv_w=conv_w, conv_b=conv_b, w_rg_a=w_rg_a, b_rg_a=b_rg_a,
        w_rg_x=w_rg_x, b_rg_x=b_rg_x, rg_lam=rg_lam, q_norm=q_norm, w_uq=w_uq, kv_norm=kv_norm, w_ukv=w_ukv,
        lam_q1=lam_q1, lam_k1=lam_k1, lam_q2=lam_q2, lam_k2=lam_k2, diff_norm=diff_norm, w_br_rnn=w_br_rnn,
        w_br_mla=w_br_mla, w_br_diff=w_br_diff, w_out=w_out))

    xp = x_prompt.reshape(nbp * sp, d)
    h0 = jnp.zeros((nbp, 2, D_RNN), F32)
    ckv_l, kr_l, dk_l, dv_l, st_l = [], [], [], [], []
    for l in range(depth):
        xp, z, ckvn, st = _layer(l, xp, mod[l, 0:1], lw, nbp, sp, nbp * sp, h0, None, None)
        ckv_l.append(ckvn.reshape(nbp, sp, KV_RANK))
        kr_l.append(z[:, C_KRP + MLA_NOPE:C_KRP + MLA_NOPE + MLA_ROPE].reshape(nbp, sp, MLA_ROPE))
        dk_l.append(z[:, C_DK:C_DK + DIFF_COLS].reshape(nbp, sp, DIFF_H, 2 * DIFF_DK))
        dv_l.append(z[:, C_DV:C_DV + DIFF_COLS].reshape(nbp, sp, DIFF_H, DIFF_DV))
        st_l.append(st)

    tabs = (_rope_tables(ss, (MLA_NOPE,), MLA_ROPE), _rope_tables(ss, (0, DIFF_DK), DIFF_DK))
    tk = min(512, ss)
    xs = x_sample.reshape(nbs * ss, d)
    for l in range(depth):
        dvt = cache_diff_v[:, l].astype(BF16).reshape(nbs, past // tk, tk, DIFF_COLS)
        ctx = (cache_mla_ckv[:, l],
               jnp.pad(cache_mla_krope[:, l], ((0, 0), (0, 0), (MLA_NOPE, HEAD_PAD - MLA_NOPE - MLA_ROPE))),
               cache_diff_k[:, l].astype(BF16).reshape(nbs * past, DIFF_COLS),
               jnp.swapaxes(dvt, 2, 3).reshape(nbs * (past // tk), DIFF_COLS, tk))
        xs, _, _, _ = _layer(l, xs, mod[l, 1:1 + nbs], lw, nbs, ss, ss, state_rnn[:, l], tabs, ctx)

    return (xp.reshape(nbp, sp, d), xs.reshape(nbs, ss, d),
            jnp.stack(ckv_l, axis=1), jnp.stack(kr_l, axis=1), jnp.stack(dk_l, axis=1), jnp.stack(dv_l, axis=1),
            jnp.stack(st_l, axis=1))
```

```python
import functools
import math

import jax
import jax.numpy as jnp
import numpy as np
from jax import lax
from jax.experimental import pallas as pl
from jax.experimental.pallas import tpu as pltpu

F32 = jnp.float32
BF16 = jnp.bfloat16

D_MODEL = 1024
GRID_W = 64
EPS = 1e-6
ROPE_THETA = 10000.0
LOG2E = math.log2(math.e)

D_RNN = 512
RNN_BLOCKS = 8
CONV_W = 4
RG_C = 8.0

MLA_H = 8
MLA_NOPE = 64
MLA_ROPE = 32
MLA_V = 64
Q_RANK = 384
KV_RANK = 256
MLA_SCALE = (MLA_NOPE + MLA_ROPE) ** -0.5

DIFF_H = 4
DIFF_DK = 64
DIFF_DV = 2 * DIFF_DK
DIFF_SCALE = DIFF_DK ** -0.5
DIFF_COLS = DIFF_H * DIFF_DV

LANES = 128
SUBLANES = 8
HEAD_PAD = LANES
TQ = 256
TK = 512
VMEM_LIMIT = 56 * 1024 * 1024

_O_CQ, _O_CKV, _O_KR, _O_MG, _O_MGATE, _O_END = 1024, 1408, 1664, 1696, 4256, 7328
C_RNN = 0
C_LAT = 1024
C_KRP = 1664
C_MG = 1792
C_QKV = 2304
C_DG = 3840
Z_COLS = 4352
ZB_MG = 0
ZB_DG = 512
ZC_KRP = 0
ZC_DK = LANES
ZC_DV = LANES + DIFF_COLS
ZC_COLS = LANES + 2 * DIFF_COLS


def _sigmoid(x):
    return 0.5 * jnp.tanh(0.5 * x) + 0.5


def _silu(x):
    return x * _sigmoid(x)


def _rms(x):
    return x * lax.rsqrt(jnp.mean(x * x, axis=-1, keepdims=True) + EPS)


def _params(*sem):
    return pltpu.CompilerParams(dimension_semantics=sem, vmem_limit_bytes=VMEM_LIMIT)


def _layer_spec(shape, layer):
    return pl.BlockSpec((None,) + tuple(shape), lambda *_: (layer,) + (0,) * len(shape))


def _mod_kernel(cond_ref, w_ref, b_ref, o_ref):
    c = cond_ref[...]
    o_ref[0] = jnp.dot(_silu(c).astype(BF16), w_ref[0].astype(BF16), preferred_element_type=F32) + b_ref[0]


def _modulation(cond, w_mod, b_mod):
    depth, d, n = w_mod.shape
    r = cond.shape[0]
    tn = 1024
    return pl.pallas_call(
        _mod_kernel,
        grid=(depth, n // tn),
        in_specs=[pl.BlockSpec((r, d), lambda l, j: (0, 0)),
                  pl.BlockSpec((1, d, tn), lambda l, j: (l, 0, j)),
                  pl.BlockSpec((1, 1, tn), lambda l, j: (l, 0, j))],
        out_specs=pl.BlockSpec((1, r, tn), lambda l, j: (l, 0, j)),
        out_shape=jax.ShapeDtypeStruct((depth, r, n), F32),
        compiler_params=_params("arbitrary", "arbitrary"),
        name="modulation",
    )(cond, w_mod, b_mod.reshape(depth, 1, n))


def _modulated_norm(x, g_ref, mod_ref):
    return (_rms(x) * g_ref[...]) * (1.0 + mod_ref[0, 1:2, :]) + mod_ref[0, 0:1, :]


def _rope(x, tab_ref, half):
    n = x.shape[-1]
    return x * tab_ref[0] + pltpu.roll(x, n - half, 1) * tab_ref[1] + pltpu.roll(x, half, 1) * tab_ref[2]


def _nt_dot(a, b):
    return lax.dot_general(a, b, (((1,), (1,)), ((), ())), preferred_element_type=F32)


def _kv_up(ckvn, krp, wuk_ref, wuvt_ref, k_ref, vt_ref):
    cb = ckvn.astype(BF16)
    kn = jnp.dot(cb, wuk_ref[...], preferred_element_type=F32)
    vt_ref[0] = _nt_dot(wuvt_ref[...], cb).astype(BF16)
    for h in range(MLA_H):
        sl = slice(h * HEAD_PAD, (h + 1) * HEAD_PAD)
        k_ref[:, sl] = (kn[:, sl] + krp).astype(BF16)


def _inproj_kernel(*refs, rope, caches):
    refs = list(refs)
    x_ref, mod_ref, g_ref, w_ref = refs[:4]
    tabm_ref, tabd_ref = (refs[4], refs[5]) if rope else (None, None)
    qn_ref, wuq_ref, kvn_ref, wuk_ref, wuvt_ref = refs[6:11] if rope else refs[4:9]
    outs = refs[11:] if rope else refs[9:]
    za_ref, zb_ref = outs[:2]
    zc_ref = outs[2] if caches else None
    q_ref, k_ref, vt_ref, ckvn_ref, qd_ref, kd_ref, vdt_ref = outs[3:] if caches else outs[2:]

    h = _modulated_norm(x_ref[...], g_ref, mod_ref).astype(BF16)
    proj = lambda c0, c1: jnp.dot(h, w_ref[:, c0:c1], preferred_element_type=F32)
    za_ref[...] = proj(C_RNN, C_LAT)

    lat = proj(C_LAT, C_MG)
    cqn = _rms(lat[:, 0:Q_RANK]) * qn_ref[...]
    q = jnp.dot(cqn.astype(BF16), wuq_ref[...], preferred_element_type=F32) * (MLA_SCALE * LOG2E)
    ckvn = _rms(lat[:, Q_RANK:Q_RANK + KV_RANK]) * kvn_ref[...]
    ckvn_ref[...] = ckvn
    krp = lat[:, C_KRP - C_LAT:C_MG - C_LAT]
    if caches:
        zc_ref[:, ZC_KRP:ZC_KRP + LANES] = krp
    if rope:
        krp = _rope(krp, tabm_ref, MLA_ROPE // 4)
    _kv_up(ckvn, krp, wuk_ref, wuvt_ref, k_ref, vt_ref)
    for hd in range(MLA_H):
        sl = slice(hd * HEAD_PAD, (hd + 1) * HEAD_PAD)
        qh = q[:, sl]
        if rope:
            qh = _rope(qh, tabm_ref, MLA_ROPE // 4)
        q_ref[:, sl] = qh.astype(BF16)
    zb_ref[:, ZB_MG:ZB_MG + MLA_H * MLA_V] = proj(C_MG, C_QKV)

    qkv = proj(C_QKV, C_DG)
    if caches:
        zc_ref[:, ZC_DK:ZC_COLS] = qkv[:, DIFF_COLS:3 * DIFF_COLS]
    for hd in range(DIFF_H):
        sl = slice(hd * LANES, (hd + 1) * LANES)
        qd = qkv[:, sl]
        kd = qkv[:, DIFF_COLS + hd * LANES:DIFF_COLS + (hd + 1) * LANES]
        if rope:
            qd = _rope(qd, tabd_ref, DIFF_DK // 4)
            kd = _rope(kd, tabd_ref, DIFF_DK // 4)
        qd_ref[:, sl] = (qd * (DIFF_SCALE * LOG2E)).astype(BF16)
        kd_ref[:, sl] = kd.astype(BF16)
        vdt_ref[0, sl, :] = qkv[:, 2 * DIFF_COLS + hd * LANES:2 * DIFF_COLS + (hd + 1) * LANES].T.astype(BF16)
    zb_ref[:, ZB_DG:ZB_DG + DIFF_COLS] = proj(C_DG, Z_COLS)


def _kv_specs(tk):
    rows = lambda width: pl.BlockSpec((tk, width), lambda i: (i, 0))
    tr = lambda height: pl.BlockSpec((1, height, tk), lambda i: (i, 0, 0))
    return [rows(MLA_H * HEAD_PAD), tr(MLA_H * MLA_V), rows(DIFF_COLS), tr(DIFF_COLS)]


def _kv_shapes(n_blocks, tk):
    return [jax.ShapeDtypeStruct((n_blocks * tk, MLA_H * HEAD_PAD), BF16),
            jax.ShapeDtypeStruct((n_blocks, MLA_H * MLA_V, tk), BF16),
            jax.ShapeDtypeStruct((n_blocks * tk, DIFF_COLS), BF16),
            jax.ShapeDtypeStruct((n_blocks, DIFF_COLS, tk), BF16)]


def _inproj(x, mod, tabs, lw, layer, seq, rows_per_mod, tk, caches):
    t, d = x.shape
    tm = tk
    tiles_per_mod = rows_per_mod // tm
    rope = tabs is not None
    in_specs = [pl.BlockSpec((tm, d), lambda i: (i, 0)),
                pl.BlockSpec((1, 3, d), lambda i: (i // tiles_per_mod, 0, 0)),
                _layer_spec((1, d), layer),
                _layer_spec((d, Z_COLS), layer)]
    args = [x, mod, lw['g_pre'], lw['w_z']]
    if rope:
        tab_spec = pl.BlockSpec((3, tm, LANES), lambda i: (0, i % (seq // tm), 0))
        in_specs += [tab_spec, tab_spec]
        args += [tabs[0], tabs[1]]
    in_specs += [_layer_spec((1, Q_RANK), layer), _layer_spec((Q_RANK, MLA_H * HEAD_PAD), layer),
                 _layer_spec((1, KV_RANK), layer), _layer_spec((KV_RANK, MLA_H * HEAD_PAD), layer),
                 _layer_spec((MLA_H * MLA_V, KV_RANK), layer)]
    args += [lw['q_norm'], lw['wuq'], lw['kv_norm'], lw['wuk'], lw['wuvt']]
    row = lambda width: pl.BlockSpec((tm, width), lambda i: (i, 0))
    f32 = lambda width: jax.ShapeDtypeStruct((t, width), F32)
    bf16 = lambda width: jax.ShapeDtypeStruct((t, width), BF16)
    k_spec, vt_spec, kd_spec, vdt_spec = _kv_specs(tk)
    k_shape, vt_shape, kd_shape, vdt_shape = _kv_shapes(t // tk, tk)
    qk = MLA_H * HEAD_PAD
    z_specs = [row(C_LAT), row(ZB_DG + DIFF_COLS)] + ([row(ZC_COLS)] if caches else [])
    z_shapes = [f32(C_LAT), f32(ZB_DG + DIFF_COLS)] + ([f32(ZC_COLS)] if caches else [])
    outs = pl.pallas_call(
        functools.partial(_inproj_kernel, rope=rope, caches=caches),
        grid=(t // tm,),
        in_specs=in_specs,
        out_specs=z_specs + [row(qk), k_spec, vt_spec, row(KV_RANK), row(DIFF_COLS), kd_spec, vdt_spec],
        out_shape=z_shapes + [bf16(qk), k_shape, vt_shape, f32(KV_RANK), bf16(DIFF_COLS), kd_shape, vdt_shape],
        compiler_params=_params("arbitrary"),
        name="inproj",
    )(*args)
    zs, (q, k, vt, ckvn, qd, kd, vdt) = outs[:len(z_specs)], outs[len(z_specs):]
    return zs, q, ckvn, qd, (k, vt, kd, vdt)


def _ctx_kernel(ckvn_ref, krp_ref, dk_ref, dv_ref, wuk_ref, wuvt_ref, k_ref, vt_ref, kd_ref, vdt_ref):
    _kv_up(ckvn_ref[...], krp_ref[...], wuk_ref, wuvt_ref, k_ref, vt_ref)
    kd_ref[...] = dk_ref[...].astype(BF16)
    for h in range(DIFF_H):
        sl = slice(h * LANES, (h + 1) * LANES)
        vdt_ref[0, sl, :] = dv_ref[:, sl].T.astype(BF16)


def _ctx(ckvn, krp, dk, dv, lw, layer, tk):
    nb, _, past, _ = ckvn.shape
    n_past = past // tk
    cached = lambda width: pl.BlockSpec((None, None, tk, width), lambda i: (i // n_past, layer, i % n_past, 0))
    return pl.pallas_call(
        _ctx_kernel,
        grid=(nb * n_past,),
        in_specs=[cached(KV_RANK), cached(LANES), cached(DIFF_COLS), cached(DIFF_COLS),
                  _layer_spec((KV_RANK, MLA_H * HEAD_PAD), layer), _layer_spec((MLA_H * MLA_V, KV_RANK), layer)],
        out_specs=_kv_specs(tk),
        out_shape=_kv_shapes(nb * n_past, tk),
        compiler_params=_params("arbitrary"),
        name="ctx_kv",
    )(ckvn, krp, dk, dv, lw['wuk'], lw['wuvt'])


def _tile_scan(a, b, reverse):
    r_in_tile = lax.broadcasted_iota(jnp.int32, a.shape, 1)
    for k in (1, 2, 4):
        shift = SUBLANES - k if reverse else k
        valid = (r_in_tile < SUBLANES - k) if reverse else (r_in_tile >= k)
        a_s = jnp.where(valid, pltpu.roll(a, shift, 1), 1.0)
        b_s = jnp.where(valid, pltpu.roll(b, shift, 1), 0.0)
        b = a * b_s + b
        a = a * a_s
    return a, b


def _rnn_kernel(rx_ref, rg_ref, h0_ref, cw_ref, cb_ref, wg_ref, bg_ref, lam_ref, y_ref, st_ref,
                xpad, hf, hb, *, seq, rows):
    n_chunks = seq // rows
    n_tiles = rows // SUBLANES
    pad = SUBLANES

    xpad[0:pad, :] = jnp.zeros((pad, D_RNN), F32)
    xpad[seq + pad:seq + 2 * pad, :] = jnp.zeros((pad, D_RNN), F32)

    def copy_chunk(c, carry):
        r0 = pl.multiple_of(c * rows, rows)
        xpad[pl.ds(r0 + pad, rows), :] = rx_ref[pl.ds(r0, rows), :]
        return carry

    lax.fori_loop(0, n_chunks, copy_chunk, 0)

    def chunk_maps(r0, d):
        n = rows + 2 * pad
        xe = xpad[pl.ds(r0, n), :]
        taps = (pltpu.roll(xe, 2, 0), pltpu.roll(xe, 1, 0), xe, pltpu.roll(xe, n - 1, 0))
        xc = cb_ref[...]
        for k in range(CONV_W):
            xc = xc + taps[k][pad:pad + rows] * cw_ref[k:k + 1, :]
        g = jnp.dot(xc.astype(BF16), wg_ref[d], preferred_element_type=F32) + bg_ref[d]
        r = _sigmoid(g[:, :D_RNN])
        i = _sigmoid(g[:, D_RNN:])
        neg_lam = -lam_ref[d]
        softplus = jnp.maximum(neg_lam, 0.0) + jnp.log1p(jnp.exp(-jnp.abs(neg_lam)))
        log_a = -RG_C * r * softplus
        a = jnp.exp(log_a)
        b = jnp.sqrt(-jnp.tanh(log_a) * (1.0 + a * a)) * (i * xc)
        return a.reshape(n_tiles, SUBLANES, D_RNN), b.reshape(n_tiles, SUBLANES, D_RNN)

    def fwd_chunk(c, carry):
        r0 = pl.multiple_of(c * rows, rows)
        a3, b3 = _tile_scan(*chunk_maps(r0, 0), reverse=False)
        for j in range(n_tiles):
            ht = a3[j] * carry + b3[j]
            hf[pl.ds(r0 + j * SUBLANES, SUBLANES), :] = ht
            carry = ht[SUBLANES - 1:SUBLANES, :]
        return carry

    def bwd_chunk(cc, carry):
        r0 = pl.multiple_of((n_chunks - 1 - cc) * rows, rows)
        a3, b3 = _tile_scan(*chunk_maps(r0, 1), reverse=True)
        for j in reversed(range(n_tiles)):
            ht = a3[j] * carry + b3[j]
            hb[j * SUBLANES:(j + 1) * SUBLANES, :] = ht
            carry = ht[0:1, :]
        y = (hf[pl.ds(r0, rows), :] + hb[...]) * _silu(rg_ref[pl.ds(r0, rows), :])
        y_ref[pl.ds(r0, rows), :] = y.astype(y_ref.dtype)
        return carry

    st_ref[0, 0:1, :] = lax.fori_loop(0, n_chunks, fwd_chunk, h0_ref[0, 0:1, :])
    st_ref[0, 1:2, :] = lax.fori_loop(0, n_chunks, bwd_chunk, h0_ref[0, 1:2, :])


def _rnn(za, h0, lw, layer, nb, seq):
    t = za.shape[0]
    rows = 256
    kern = functools.partial(_rnn_kernel, seq=seq, rows=rows)
    return pl.pallas_call(
        kern,
        grid=(nb,),
        in_specs=[pl.BlockSpec((seq, D_RNN), lambda b: (b, 0)),
                  pl.BlockSpec((seq, D_RNN), lambda b: (b, 1)),
                  pl.BlockSpec((1, 2, D_RNN), lambda b: (b, 0, 0)),
                  _layer_spec((CONV_W, D_RNN), layer),
                  _layer_spec((1, D_RNN), layer),
                  _layer_spec((2, D_RNN, 2 * D_RNN), layer),
                  _layer_spec((2, 1, 2 * D_RNN), layer),
                  _layer_spec((2, 1, D_RNN), layer)],
        out_specs=[pl.BlockSpec((seq, D_RNN), lambda b: (b, 0)),
                   pl.BlockSpec((1, 2, D_RNN), lambda b: (b, 0, 0))],
        out_shape=[jax.ShapeDtypeStruct((t, D_RNN), BF16),
                   jax.ShapeDtypeStruct((nb, 2, D_RNN), F32)],
        scratch_shapes=[pltpu.VMEM((seq + 2 * SUBLANES, D_RNN), F32),
                        pltpu.VMEM((seq, D_RNN), F32),
                        pltpu.VMEM((rows, D_RNN), F32)],
        compiler_params=_params("arbitrary"),
        name="rglru",
    )(za, za, h0, lw['conv_w'], lw['conv_b'], lw['wg'], lw['bg'], lw['lam'])


def _attend_t(chains, kv_refs, tk):
    blocks = [(k_ref, vt_ref, j) for k_ref, vt_ref in kv_refs for j in range(vt_ref.shape[0])]

    def scores(block):
        k_ref, _, j = block
        k_blocks = {}
        out = []
        for q, kcols, _ in chains:
            key = (kcols.start, kcols.stop)
            if key not in k_blocks:
                k_blocks[key] = k_ref[j * tk:(j + 1) * tk, kcols]
            out.append(_nt_dot(k_blocks[key], q))
        return out

    state = [None] * len(chains)
    s_cur = scores(blocks[0])
    for n, (_, vt_ref, j) in enumerate(blocks):
        s_next = scores(blocks[n + 1]) if n + 1 < len(blocks) else None
        for c, (_, _, vrows) in enumerate(chains):
            st = s_cur[c]
            m_blk = jnp.max(st, axis=0, keepdims=True)
            if state[c] is None:
                m_new = m_blk
            else:
                m, l, acc = state[c]
                m_new = jnp.maximum(m, m_blk)
                alpha = jnp.exp2(m - m_new)
            p = jnp.exp2(st - m_new)
            l_blk = jnp.sum(p, axis=0, keepdims=True)
            pv = jnp.dot(vt_ref[j, vrows, :], p.astype(BF16), preferred_element_type=F32)
            state[c] = (m_new, l_blk, pv) if state[c] is None else (m_new, alpha * l + l_blk, alpha * acc + pv)
        s_cur = s_next
    return [acc / l for _, l, acc in state]


def _kv_in_specs(kv_parts, nb, tk, k_width, vt_rows):
    specs, args = [], []
    for k, vt in kv_parts:
        n = vt.shape[0] // nb
        specs += [pl.BlockSpec((n * tk, k_width), lambda b, h, i: (b, h)),
                  pl.BlockSpec((n, vt_rows, tk), lambda b, h, i: (b, h, 0))]
        args += [k, vt]
    return specs, args


def _gate_specs(col0, width, tq, row_block):
    w = math.gcd(col0, width) if col0 else width
    return [pl.BlockSpec((tq, w), lambda b, h, i, n=n: (row_block(b, i), (col0 + h * width) // w + n))
            for n in range(width // w)]


def _gate(refs, rows):
    return jnp.concatenate([r[rows, :] for r in refs], axis=1)


def _mla_attn_kernel(q_ref, *refs, tk, n_parts, n_sub, n_heads):
    kv_refs = [(refs[2 * n], refs[2 * n + 1]) for n in range(n_parts)]
    mg_refs, o_ref = refs[2 * n_parts:-1], refs[-1]
    chains = [(q_ref[s * TQ:(s + 1) * TQ, h * HEAD_PAD:(h + 1) * HEAD_PAD],
               slice(h * HEAD_PAD, (h + 1) * HEAD_PAD), slice(h * MLA_V, (h + 1) * MLA_V))
              for s in range(n_sub) for h in range(n_heads)]
    outs = _attend_t(chains, kv_refs, tk)
    for s in range(n_sub):
        rows = slice(s * TQ, (s + 1) * TQ)
        ot = jnp.concatenate(outs[s * n_heads:(s + 1) * n_heads], axis=0)
        o_ref[rows, :] = (ot.T * _silu(_gate(mg_refs, rows))).astype(o_ref.dtype)


def _attn_tiling(seq, n_kv, heads):
    if n_kv == 1:
        return 1, heads
    return min(2, seq // TQ), None


def _mla_attn(q, kv_parts, zb, nb, seq, tk):
    t = q.shape[0]
    n_kv = sum(vt.shape[0] for _, vt in kv_parts) // nb
    n_sub, n_heads = _attn_tiling(seq, n_kv, MLA_H)
    n_heads = n_heads or 2
    tq = n_sub * TQ
    nq = seq // tq
    kern = functools.partial(_mla_attn_kernel, tk=tk, n_parts=len(kv_parts), n_sub=n_sub, n_heads=n_heads)
    kv_specs, kv_args = _kv_in_specs(kv_parts, nb, tk, n_heads * HEAD_PAD, n_heads * MLA_V)
    mg_specs = _gate_specs(ZB_MG, n_heads * MLA_V, tq, lambda b, i: b * nq + i)
    return pl.pallas_call(
        kern,
        grid=(nb, MLA_H // n_heads, nq),
        in_specs=[pl.BlockSpec((tq, n_heads * HEAD_PAD), lambda b, h, i: (b * nq + i, h))] + kv_specs + mg_specs,
        out_specs=pl.BlockSpec((tq, n_heads * MLA_V), lambda b, h, i: (b * nq + i, h)),
        out_shape=jax.ShapeDtypeStruct((t, MLA_H * MLA_V), BF16),
        compiler_params=_params("arbitrary", "arbitrary", "arbitrary"),
        name="mla_attn",
    )(q, *kv_args, *([zb] * len(mg_specs)))


def _diff_attn_kernel(q_ref, lam_ref, dn_ref, *refs, tk, n_parts, n_sub, n_heads, lam_init):
    kv_refs = [(refs[2 * n], refs[2 * n + 1]) for n in range(n_parts)]
    dg_refs, o_ref = refs[2 * n_parts:-1], refs[-1]
    lp = lam_ref[...]
    lam = (jnp.exp(jnp.sum(lp[0:1] * lp[1:2], axis=-1, keepdims=True))
           - jnp.exp(jnp.sum(lp[2:3] * lp[3:4], axis=-1, keepdims=True)) + lam_init)
    chains = []
    for s in range(n_sub):
        for h in range(n_heads):
            cols = slice(h * LANES, (h + 1) * LANES)
            q = q_ref[s * TQ:(s + 1) * TQ, cols].astype(F32)
            lane = lax.broadcasted_iota(jnp.int32, q.shape, 1)
            chains.append((jnp.where(lane < DIFF_DK, q, 0.0).astype(BF16), cols, cols))
            chains.append((jnp.where(lane < DIFF_DK, 0.0, q).astype(BF16), cols, cols))
    outs = _attend_t(chains, kv_refs, tk)
    for s in range(n_sub):
        rows = slice(s * TQ, (s + 1) * TQ)
        heads = []
        for h in range(n_heads):
            c = 2 * (s * n_heads + h)
            od = (outs[c] - lam * outs[c + 1]).T
            heads.append(_rms(od) * dn_ref[...] * (1.0 - lam_init))
        od = heads[0] if n_heads == 1 else jnp.concatenate(heads, axis=1)
        o_ref[rows, :] = (od * _silu(_gate(dg_refs, rows))).astype(o_ref.dtype)


def _diff_attn(qd, kv_parts, zb, lw, layer, nb, seq, tk):
    t = qd.shape[0]
    n_kv = sum(vt.shape[0] for _, vt in kv_parts) // nb
    n_sub, n_heads = _attn_tiling(seq, n_kv, DIFF_H)
    n_heads = n_heads or 1
    tq = n_sub * TQ
    nq = seq // tq
    lam_init = 0.8 - 0.6 * math.exp(-0.3 * layer)
    kern = functools.partial(_diff_attn_kernel, tk=tk, n_parts=len(kv_parts), n_sub=n_sub, n_heads=n_heads,
                             lam_init=lam_init)
    width = n_heads * LANES
    kv_specs, kv_args = _kv_in_specs(kv_parts, nb, tk, width, width)
    dg_specs = _gate_specs(ZB_DG, width, tq, lambda b, i: b * nq + i)
    return pl.pallas_call(
        kern,
        grid=(nb, DIFF_H // n_heads, nq),
        in_specs=[pl.BlockSpec((tq, width), lambda b, h, i: (b * nq + i, h)),
                  _layer_spec((4, DIFF_DK), layer),
                  _layer_spec((1, DIFF_DV), layer)] + kv_specs + dg_specs,
        out_specs=pl.BlockSpec((tq, width), lambda b, h, i: (b * nq + i, h)),
        out_shape=jax.ShapeDtypeStruct((t, DIFF_COLS), BF16),
        compiler_params=_params("arbitrary", "arbitrary", "arbitrary"),
        name="diff_attn",
    )(qd, lw['lam_p'], lw['diff_norm'], *kv_args, *([zb] * len(dg_specs)))


def _merge_kernel(yr_ref, ym_ref, yd_ref, x_ref, mod_ref, gpre_ref, wg_ref, wr_ref, wm_ref, wd_ref, wo_ref,
                  gpost_ref, o_ref):
    x = x_ref[...]
    h = _modulated_norm(x, gpre_ref, mod_ref).astype(BF16)
    merged = None
    for k, (y_ref, w_ref) in enumerate(((yr_ref, wr_ref), (ym_ref, wm_ref), (yd_ref, wd_ref))):
        gate = _sigmoid(jnp.dot(h, wg_ref[:, k * D_MODEL:(k + 1) * D_MODEL], preferred_element_type=F32))
        term = gate * jnp.dot(y_ref[...], w_ref[...], preferred_element_type=F32)
        merged = term if merged is None else merged + term
    o = jnp.dot(merged.astype(BF16), wo_ref[...], preferred_element_type=F32)
    o_ref[...] = x + mod_ref[0, 2:3, :] * (_rms(o) * gpost_ref[...])


def _merge(y_rnn, y_mla, y_diff, x, mod, lw, layer, rows_per_mod):
    t, d = x.shape
    tm = 512
    tiles_per_mod = rows_per_mod // tm
    row = lambda width: pl.BlockSpec((tm, width), lambda i: (i, 0))
    return pl.pallas_call(
        _merge_kernel,
        grid=(t // tm,),
        in_specs=[row(D_RNN), row(MLA_H * MLA_V), row(DIFF_COLS), row(d),
                  pl.BlockSpec((1, 3, d), lambda i: (i // tiles_per_mod, 0, 0)),
                  _layer_spec((1, d), layer), _layer_spec((d, 3 * d), layer),
                  _layer_spec((D_RNN, d), layer), _layer_spec((MLA_H * MLA_V, d), layer),
                  _layer_spec((DIFF_COLS, d), layer), _layer_spec((d, d), layer), _layer_spec((1, d), layer)],
        out_specs=row(d),
        out_shape=jax.ShapeDtypeStruct((t, d), F32),
        compiler_params=_params("arbitrary"),
        name="merge_out",
    )(y_rnn, y_mla, y_diff, x, mod, lw['g_pre'], lw['w_mgate'], lw['w_br_rnn'], lw['w_br_mla'], lw['w_br_diff'],
      lw['w_out'], lw['g_post'])


def _pad_heads(w, width):
    l, k, _ = w.shape
    w4 = jnp.pad(w.reshape(l, k, MLA_H, width), ((0, 0), (0, 0), (0, 0), (0, HEAD_PAD - width)))
    return w4.reshape(l, k, MLA_H * HEAD_PAD)


def _w_in_kernel(w_ref, wz_ref, wg_ref):
    w = w_ref[0]
    rows = w.shape[0]
    wz_ref[0, :, 0:_O_KR] = w[:, 0:_O_KR].astype(BF16)
    wz_ref[0, :, C_KRP:C_KRP + LANES] = jnp.zeros((rows, LANES), BF16)
    wz_ref[0, :, C_KRP + MLA_NOPE:C_KRP + MLA_NOPE + MLA_ROPE] = w[:, _O_KR:_O_MG].astype(BF16)
    wz_ref[0, :, C_MG:Z_COLS] = w[:, _O_MG:_O_MGATE].astype(BF16)
    wg_ref[0] = w[:, _O_MGATE:_O_END].astype(BF16)


def _split_w_in(w_in):
    depth, d, n = w_in.shape
    tr = 256
    return pl.pallas_call(
        _w_in_kernel,
        grid=(depth, d // tr),
        in_specs=[pl.BlockSpec((1, tr, n), lambda l, i: (l, i, 0))],
        out_specs=[pl.BlockSpec((1, tr, Z_COLS), lambda l, i: (l, i, 0)),
                   pl.BlockSpec((1, tr, _O_END - _O_MGATE), lambda l, i: (l, i, 0))],
        out_shape=[jax.ShapeDtypeStruct((depth, d, Z_COLS), BF16),
                   jax.ShapeDtypeStruct((depth, d, _O_END - _O_MGATE), BF16)],
        compiler_params=_params("arbitrary", "arbitrary"),
        name="w_in_layout",
    )(w_in)


def _layer_params(p):
    depth, d, _ = p['w_in'].shape
    w_z, w_mgate = _split_w_in(p['w_in'])
    eye = jnp.eye(RNN_BLOCKS, dtype=F32)
    dense = lambda w: jnp.einsum('ldnkj,nm->ldnkmj', w, eye).reshape(depth, 2, D_RNN, D_RNN)
    ukv = p['w_ukv'].reshape(depth, KV_RANK, MLA_H, MLA_NOPE + MLA_V)
    return dict(
        g_pre=p['g_pre'].reshape(depth, 1, d), g_post=p['g_post'].reshape(depth, 1, d),
        w_z=w_z, w_mgate=w_mgate,
        conv_w=p['conv_w'], conv_b=p['conv_b'].reshape(depth, 1, D_RNN),
        wg=jnp.concatenate([dense(p['w_rg_a']), dense(p['w_rg_x'])], axis=-1).astype(BF16),
        bg=jnp.concatenate([p['b_rg_a'], p['b_rg_x']], axis=-1).reshape(depth, 2, 1, 2 * D_RNN),
        lam=p['rg_lam'].reshape(depth, 2, 1, D_RNN),
        q_norm=p['q_norm'].reshape(depth, 1, Q_RANK), kv_norm=p['kv_norm'].reshape(depth, 1, KV_RANK),
        wuq=_pad_heads(p['w_uq'], MLA_NOPE + MLA_ROPE).astype(BF16),
        wuk=_pad_heads(ukv[..., :MLA_NOPE].reshape(depth, KV_RANK, MLA_H * MLA_NOPE), MLA_NOPE).astype(BF16),
        wuvt=jnp.swapaxes(ukv[..., MLA_NOPE:].reshape(depth, KV_RANK, MLA_H * MLA_V), 1, 2).astype(BF16),
        lam_p=jnp.stack([p['lam_q1'], p['lam_k1'], p['lam_q2'], p['lam_k2']], axis=1),
        diff_norm=p['diff_norm'].reshape(depth, 1, DIFF_DV),
        w_br_rnn=p['w_br_rnn'].astype(BF16), w_br_mla=p['w_br_mla'].astype(BF16),
        w_br_diff=p['w_br_diff'].astype(BF16), w_out=p['w_out'].astype(BF16))


def _rope_tables(n_pos, groups, dim):
    half = dim // 4
    inv = ROPE_THETA ** (-np.arange(half, dtype=np.float32) / half)
    inv_row = np.zeros((LANES,), np.float32)
    inv_col = np.zeros((LANES,), np.float32)
    first = np.zeros((LANES,), np.float32)
    second = np.zeros((LANES,), np.float32)
    for off in groups:
        for inv_axis, base in ((inv_row, off), (inv_col, off + 2 * half)):
            inv_axis[base:base + half] = inv
            inv_axis[base + half:base + 2 * half] = inv
            first[base:base + half] = 1.0
            second[base + half:base + 2 * half] = 1.0
    pos = jnp.arange(n_pos, dtype=jnp.int32)
    ang = ((pos // GRID_W).astype(F32)[:, None] * jnp.asarray(inv_row)[None, :]
           + (pos % GRID_W).astype(F32)[:, None] * jnp.asarray(inv_col)[None, :])
    sin = jnp.sin(ang)
    return jnp.stack([jnp.cos(ang), -sin * jnp.asarray(first), sin * jnp.asarray(second)])


def _layer(l, x, mod, lw, nb, seq, rows_per_mod, h0, tabs, ctx):
    tk = min(TK, seq)
    zs, q, ckvn, qd, (k, vt, kd, vdt) = _inproj(x, mod, tabs, lw, l, seq, rows_per_mod, tk, caches=ctx is None)
    y_rnn, st = _rnn(zs[0], h0, lw, l, nb, seq)
    mla_kv, diff_kv = [(k, vt)], [(kd, vdt)]
    if ctx is not None:
        kc, vtc, kdc, vdtc = _ctx(*ctx, lw, l, tk)
        mla_kv, diff_kv = [(kc, vtc)] + mla_kv, [(kdc, vdtc)] + diff_kv
    y_mla = _mla_attn(q, mla_kv, zs[1], nb, seq, tk)
    y_diff = _diff_attn(qd, diff_kv, zs[1], lw, l, nb, seq, tk)
    x_new = _merge(y_rnn, y_mla, y_diff, x, mod, lw, l, rows_per_mod)
    return x_new, (zs[2] if ctx is None else None), ckvn, st


def kernel(x_prompt, x_sample, cache_mla_ckv, cache_mla_krope, cache_diff_k, cache_diff_v, state_rnn, c, c_ctx, w_mod, b_mod, g_pre, g_post, w_in, conv_w, conv_b, w_rg_a, b_rg_a, w_rg_x, b_rg_x, rg_lam, q_norm, w_uq, kv_norm, w_ukv, lam_q1, lam_k1, lam_q2, lam_k2, diff_norm, w_br_rnn, w_br_mla, w_br_diff, w_out):
    depth = w_in.shape[0]
    nbp, sp, d = x_prompt.shape
    nbs, ss, _ = x_sample.shape
    past = cache_mla_ckv.shape[2]

    n_cond = -(-(1 + nbs) // SUBLANES) * SUBLANES
    cond = jnp.concatenate([c_ctx[None, :], c, jnp.zeros((n_cond - 1 - nbs, d), F32)], axis=0)
    mod = _modulation(cond, w_mod, b_mod).reshape(depth, n_cond, 3, d)

    lw = _layer_params(dict(
        w_in=w_in, g_pre=g_pre, g_post=g_post, conv_w=conv_w, conv_b=conv_b, w_rg_a=w_rg_a, b_rg_a=b_rg_a,
        w_rg_x=w_rg_x, b_rg_x=b_rg_x, rg_lam=rg_lam, q_norm=q_norm, w_uq=w_uq, kv_norm=kv_norm, w_ukv=w_ukv,
        lam_q1=lam_q1, lam_k1=lam_k1, lam_q2=lam_q2, lam_k2=lam_k2, diff_norm=diff_norm, w_br_rnn=w_br_rnn,
        w_br_mla=w_br_mla, w_br_diff=w_br_diff, w_out=w_out))

    xp = x_prompt.reshape(nbp * sp, d)
    h0 = jnp.zeros((nbp, 2, D_RNN), F32)
    ckv_l, kr_l, dk_l, dv_l, st_l = [], [], [], [], []
    for l in range(depth):
        xp, zc, ckvn, st = _layer(l, xp, mod[l, 0:1], lw, nbp, sp, nbp * sp, h0, None, None)
        ckv_l.append(ckvn.reshape(nbp, sp, KV_RANK))
        kr_l.append(zc[:, ZC_KRP + MLA_NOPE:ZC_KRP + MLA_NOPE + MLA_ROPE].reshape(nbp, sp, MLA_ROPE))
        dk_l.append(zc[:, ZC_DK:ZC_DK + DIFF_COLS].reshape(nbp, sp, DIFF_H, 2 * DIFF_DK))
        dv_l.append(zc[:, ZC_DV:ZC_DV + DIFF_COLS].reshape(nbp, sp, DIFF_H, DIFF_DV))
        st_l.append(st)

    tabs = (_rope_tables(ss, (MLA_NOPE,), MLA_ROPE), _rope_tables(ss, (0, DIFF_DK), DIFF_DK))
    depth_c = cache_mla_ckv.shape[1]
    ctx = (cache_mla_ckv,
           jnp.pad(cache_mla_krope, ((0, 0), (0, 0), (0, 0), (MLA_NOPE, HEAD_PAD - MLA_NOPE - MLA_ROPE))),
           cache_diff_k.reshape(nbs, depth_c, past, DIFF_COLS), cache_diff_v.reshape(nbs, depth_c, past, DIFF_COLS))
    xs = x_sample.reshape(nbs * ss, d)
    for l in range(depth):
        xs, _, _, _ = _layer(l, xs, mod[l, 1:1 + nbs], lw, nbs, ss, ss, state_rnn[:, l], tabs, ctx)

    return (xp.reshape(nbp, sp, d), xs.reshape(nbs, ss, d),
            jnp.stack(ckv_l, axis=1), jnp.stack(kr_l, axis=1), jnp.stack(dk_l, axis=1), jnp.stack(dv_l, axis=1),
            jnp.stack(st_l, axis=1))
```

```python
import functools
import math

import jax
import jax.numpy as jnp
import numpy as np
from jax import lax
from jax.experimental import pallas as pl
from jax.experimental.pallas import tpu as pltpu

F32 = jnp.float32
BF16 = jnp.bfloat16

D_MODEL = 1024
GRID_W = 64
EPS = 1e-6
ROPE_THETA = 10000.0
LOG2E = math.log2(math.e)

D_RNN = 512
RNN_BLOCKS = 8
CONV_W = 4
RG_C = 8.0

MLA_H = 8
MLA_NOPE = 64
MLA_ROPE = 32
MLA_V = 64
Q_RANK = 384
KV_RANK = 256
MLA_SCALE = (MLA_NOPE + MLA_ROPE) ** -0.5

DIFF_H = 4
DIFF_DK = 64
DIFF_DV = 2 * DIFF_DK
DIFF_SCALE = DIFF_DK ** -0.5
DIFF_COLS = DIFF_H * DIFF_DV

LANES = 128
SUBLANES = 8
HEAD_PAD = LANES
TQ = 256
TK = 512
VMEM_LIMIT = 56 * 1024 * 1024

_O_CQ, _O_CKV, _O_KR, _O_MG, _O_MGATE, _O_END = 1024, 1408, 1664, 1696, 4256, 7328
C_RNN = 0
C_LAT = 1024
C_KRP = 1664
C_MG = 1792
C_QKV = 2304
C_DG = 3840
Z_COLS = 4352
ZB_MG = 0
ZB_DG = 512


def _sigmoid(x):
    return 0.5 * jnp.tanh(0.5 * x) + 0.5


def _silu(x):
    return x * _sigmoid(x)


def _rms(x):
    return x * lax.rsqrt(jnp.mean(x * x, axis=-1, keepdims=True) + EPS)


def _params(*sem):
    return pltpu.CompilerParams(dimension_semantics=sem, vmem_limit_bytes=VMEM_LIMIT)


def _layer_spec(shape, layer):
    return pl.BlockSpec((None,) + tuple(shape), lambda *_: (layer,) + (0,) * len(shape))


def _mod_kernel(cond_ref, w_ref, b_ref, o_ref):
    c = cond_ref[...]
    o_ref[0] = jnp.dot(_silu(c).astype(BF16), w_ref[0].astype(BF16), preferred_element_type=F32) + b_ref[0]


def _modulation(cond, w_mod, b_mod):
    depth, d, n = w_mod.shape
    r = cond.shape[0]
    tn = 1024
    return pl.pallas_call(
        _mod_kernel,
        grid=(depth, n // tn),
        in_specs=[pl.BlockSpec((r, d), lambda l, j: (0, 0)),
                  pl.BlockSpec((1, d, tn), lambda l, j: (l, 0, j)),
                  pl.BlockSpec((1, 1, tn), lambda l, j: (l, 0, j))],
        out_specs=pl.BlockSpec((1, r, tn), lambda l, j: (l, 0, j)),
        out_shape=jax.ShapeDtypeStruct((depth, r, n), F32),
        compiler_params=_params("arbitrary", "arbitrary"),
        name="modulation",
    )(cond, w_mod, b_mod.reshape(depth, 1, n))


def _modulated_norm(x, g_ref, mod_ref):
    return (_rms(x) * g_ref[...]) * (1.0 + mod_ref[0, 1:2, :]) + mod_ref[0, 0:1, :]


def _rope(x, tab_ref, half):
    n = x.shape[-1]
    return x * tab_ref[0] + pltpu.roll(x, n - half, 1) * tab_ref[1] + pltpu.roll(x, half, 1) * tab_ref[2]


def _nt_dot(a, b):
    return lax.dot_general(a, b, (((1,), (1,)), ((), ())), preferred_element_type=F32)


def _kv_up(ckvn, krp, wuk_ref, wuvt_ref, k_ref, vt_ref):
    cb = ckvn.astype(BF16)
    kn = jnp.dot(cb, wuk_ref[...], preferred_element_type=F32)
    vt_ref[0] = _nt_dot(wuvt_ref[...], cb).astype(BF16)
    for h in range(MLA_H):
        sl = slice(h * HEAD_PAD, (h + 1) * HEAD_PAD)
        k_ref[:, sl] = (kn[:, sl] + krp).astype(BF16)


def _inproj_kernel(*refs, rope, caches):
    refs = list(refs)
    x_ref, mod_ref, g_ref, w_ref = refs[:4]
    tabm_ref, tabd_ref = (refs[4], refs[5]) if rope else (None, None)
    qn_ref, wuq_ref, kvn_ref, wuk_ref, wuvt_ref = refs[6:11] if rope else refs[4:9]
    outs = refs[11:] if rope else refs[9:]
    za_ref, zb_ref = outs[:2]
    krc_ref, dkc_ref, dvc_ref = outs[2:5] if caches else (None, None, None)
    q_ref, k_ref, vt_ref, ckvn_ref, qd_ref, kd_ref, vdt_ref = outs[5:] if caches else outs[2:]

    h = _modulated_norm(x_ref[...], g_ref, mod_ref).astype(BF16)
    proj = lambda c0, c1: jnp.dot(h, w_ref[:, c0:c1], preferred_element_type=F32)
    za_ref[...] = proj(C_RNN, C_LAT)

    lat = proj(C_LAT, C_MG)
    cqn = _rms(lat[:, 0:Q_RANK]) * qn_ref[...]
    q = jnp.dot(cqn.astype(BF16), wuq_ref[...], preferred_element_type=F32) * (MLA_SCALE * LOG2E)
    ckvn = _rms(lat[:, Q_RANK:Q_RANK + KV_RANK]) * kvn_ref[...]
    ckvn_ref[...] = ckvn
    krp = lat[:, C_KRP - C_LAT:C_MG - C_LAT]
    if caches:
        krc_ref[...] = krp[:, MLA_NOPE:MLA_NOPE + MLA_ROPE]
    if rope:
        krp = _rope(krp, tabm_ref, MLA_ROPE // 4)
    _kv_up(ckvn, krp, wuk_ref, wuvt_ref, k_ref, vt_ref)
    for hd in range(MLA_H):
        sl = slice(hd * HEAD_PAD, (hd + 1) * HEAD_PAD)
        qh = q[:, sl]
        if rope:
            qh = _rope(qh, tabm_ref, MLA_ROPE // 4)
        q_ref[:, sl] = qh.astype(BF16)
    zb_ref[:, ZB_MG:ZB_MG + MLA_H * MLA_V] = proj(C_MG, C_QKV)

    qkv = proj(C_QKV, C_DG)
    for hd in range(DIFF_H):
        sl = slice(hd * LANES, (hd + 1) * LANES)
        qd = qkv[:, sl]
        kd = qkv[:, DIFF_COLS + hd * LANES:DIFF_COLS + (hd + 1) * LANES]
        vd = qkv[:, 2 * DIFF_COLS + hd * LANES:2 * DIFF_COLS + (hd + 1) * LANES]
        if caches:
            dkc_ref[:, hd, :] = kd
            dvc_ref[:, hd, :] = vd
        if rope:
            qd = _rope(qd, tabd_ref, DIFF_DK // 4)
            kd = _rope(kd, tabd_ref, DIFF_DK // 4)
        qd_ref[:, sl] = (qd * (DIFF_SCALE * LOG2E)).astype(BF16)
        kd_ref[:, sl] = kd.astype(BF16)
        vdt_ref[0, sl, :] = vd.T.astype(BF16)
    zb_ref[:, ZB_DG:ZB_DG + DIFF_COLS] = proj(C_DG, Z_COLS)


def _kv_specs(tk):
    rows = lambda width: pl.BlockSpec((tk, width), lambda i: (i, 0))
    tr = lambda height: pl.BlockSpec((1, height, tk), lambda i: (i, 0, 0))
    return [rows(MLA_H * HEAD_PAD), tr(MLA_H * MLA_V), rows(DIFF_COLS), tr(DIFF_COLS)]


def _kv_shapes(n_blocks, tk):
    return [jax.ShapeDtypeStruct((n_blocks * tk, MLA_H * HEAD_PAD), BF16),
            jax.ShapeDtypeStruct((n_blocks, MLA_H * MLA_V, tk), BF16),
            jax.ShapeDtypeStruct((n_blocks * tk, DIFF_COLS), BF16),
            jax.ShapeDtypeStruct((n_blocks, DIFF_COLS, tk), BF16)]


def _inproj(x, mod, tabs, lw, layer, seq, rows_per_mod, tk, caches):
    t, d = x.shape
    tm = tk
    tiles_per_mod = rows_per_mod // tm
    rope = tabs is not None
    in_specs = [pl.BlockSpec((tm, d), lambda i: (i, 0)),
                pl.BlockSpec((1, 3, d), lambda i: (i // tiles_per_mod, 0, 0)),
                _layer_spec((1, d), layer),
                _layer_spec((d, Z_COLS), layer)]
    args = [x, mod, lw['g_pre'], lw['w_z']]
    if rope:
        tab_spec = pl.BlockSpec((3, tm, LANES), lambda i: (0, i % (seq // tm), 0))
        in_specs += [tab_spec, tab_spec]
        args += [tabs[0], tabs[1]]
    in_specs += [_layer_spec((1, Q_RANK), layer), _layer_spec((Q_RANK, MLA_H * HEAD_PAD), layer),
                 _layer_spec((1, KV_RANK), layer), _layer_spec((KV_RANK, MLA_H * HEAD_PAD), layer),
                 _layer_spec((MLA_H * MLA_V, KV_RANK), layer)]
    args += [lw['q_norm'], lw['wuq'], lw['kv_norm'], lw['wuk'], lw['wuvt']]
    row = lambda width: pl.BlockSpec((tm, width), lambda i: (i, 0))
    f32 = lambda width: jax.ShapeDtypeStruct((t, width), F32)
    bf16 = lambda width: jax.ShapeDtypeStruct((t, width), BF16)
    k_spec, vt_spec, kd_spec, vdt_spec = _kv_specs(tk)
    k_shape, vt_shape, kd_shape, vdt_shape = _kv_shapes(t // tk, tk)
    qk = MLA_H * HEAD_PAD
    z_specs = [row(C_LAT), row(ZB_DG + DIFF_COLS)]
    z_shapes = [f32(C_LAT), f32(ZB_DG + DIFF_COLS)]
    if caches:
        heads = pl.BlockSpec((tm, DIFF_H, DIFF_DV), lambda i: (i, 0, 0))
        z_specs += [row(MLA_ROPE), heads, heads]
        z_shapes += [f32(MLA_ROPE)] + [jax.ShapeDtypeStruct((t, DIFF_H, DIFF_DV), F32)] * 2
    outs = pl.pallas_call(
        functools.partial(_inproj_kernel, rope=rope, caches=caches),
        grid=(t // tm,),
        in_specs=in_specs,
        out_specs=z_specs + [row(qk), k_spec, vt_spec, row(KV_RANK), row(DIFF_COLS), kd_spec, vdt_spec],
        out_shape=z_shapes + [bf16(qk), k_shape, vt_shape, f32(KV_RANK), bf16(DIFF_COLS), kd_shape, vdt_shape],
        compiler_params=_params("arbitrary"),
        name="inproj",
    )(*args)
    zs, (q, k, vt, ckvn, qd, kd, vdt) = outs[:len(z_specs)], outs[len(z_specs):]
    return zs, q, ckvn, qd, (k, vt, kd, vdt)


def _ctx_kernel(ckvn_ref, kr_ref, dk_ref, dv_ref, wuk_ref, wuvt_ref, k_ref, vt_ref, kd_ref, vdt_ref):
    row = lax.broadcasted_iota(jnp.int32, (MLA_ROPE, LANES), 0)
    lane = lax.broadcasted_iota(jnp.int32, (MLA_ROPE, LANES), 1)
    place = (lane == row + MLA_NOPE).astype(BF16)
    krp = jnp.dot(kr_ref[...].astype(BF16), place, preferred_element_type=F32)
    _kv_up(ckvn_ref[...], krp, wuk_ref, wuvt_ref, k_ref, vt_ref)
    for h in range(DIFF_H):
        sl = slice(h * LANES, (h + 1) * LANES)
        kd_ref[:, sl] = dk_ref[:, h, :].astype(BF16)
        vdt_ref[0, sl, :] = dv_ref[:, h, :].T.astype(BF16)


def _ctx(ckvn, kr, dk, dv, lw, layer, tk):
    nb, _, past, _ = ckvn.shape
    n_past = past // tk
    cached = lambda *minor: pl.BlockSpec((None, None, tk) + minor,
                                         lambda i: (i // n_past, layer, i % n_past) + (0,) * len(minor))
    return pl.pallas_call(
        _ctx_kernel,
        grid=(nb * n_past,),
        in_specs=[cached(KV_RANK), cached(MLA_ROPE), cached(DIFF_H, DIFF_DV), cached(DIFF_H, DIFF_DV),
                  _layer_spec((KV_RANK, MLA_H * HEAD_PAD), layer), _layer_spec((MLA_H * MLA_V, KV_RANK), layer)],
        out_specs=_kv_specs(tk),
        out_shape=_kv_shapes(nb * n_past, tk),
        compiler_params=_params("arbitrary"),
        name="ctx_kv",
    )(ckvn, kr, dk, dv, lw['wuk'], lw['wuvt'])


def _tile_scan(a, b, reverse):
    r_in_tile = lax.broadcasted_iota(jnp.int32, a.shape, 1)
    for k in (1, 2, 4):
        shift = SUBLANES - k if reverse else k
        valid = (r_in_tile < SUBLANES - k) if reverse else (r_in_tile >= k)
        a_s = jnp.where(valid, pltpu.roll(a, shift, 1), 1.0)
        b_s = jnp.where(valid, pltpu.roll(b, shift, 1), 0.0)
        b = a * b_s + b
        a = a * a_s
    return a, b


def _rnn_kernel(rx_ref, rg_ref, h0_ref, cw_ref, cb_ref, wg_ref, bg_ref, lam_ref, y_ref, st_ref,
                xpad, hf, hb, *, seq, rows):
    n_chunks = seq // rows
    n_tiles = rows // SUBLANES
    pad = SUBLANES

    xpad[0:pad, :] = jnp.zeros((pad, D_RNN), F32)
    xpad[seq + pad:seq + 2 * pad, :] = jnp.zeros((pad, D_RNN), F32)

    def copy_chunk(c, carry):
        r0 = pl.multiple_of(c * rows, rows)
        xpad[pl.ds(r0 + pad, rows), :] = rx_ref[pl.ds(r0, rows), :]
        return carry

    lax.fori_loop(0, n_chunks, copy_chunk, 0)

    def chunk_maps(r0, d):
        n = rows + 2 * pad
        xe = xpad[pl.ds(r0, n), :]
        taps = (pltpu.roll(xe, 2, 0), pltpu.roll(xe, 1, 0), xe, pltpu.roll(xe, n - 1, 0))
        xc = cb_ref[...]
        for k in range(CONV_W):
            xc = xc + taps[k][pad:pad + rows] * cw_ref[k:k + 1, :]
        t = jnp.tanh(jnp.dot(xc.astype(BF16), wg_ref[d], preferred_element_type=F32) + bg_ref[d])
        i = 0.5 * t[:, D_RNN:] + 0.5
        neg_lam = -lam_ref[d]
        softplus = jnp.maximum(neg_lam, 0.0) + jnp.log1p(jnp.exp(-jnp.abs(neg_lam)))
        c = (-0.5 * RG_C) * softplus
        log_a = c * t[:, :D_RNN] + c
        a = jnp.exp(log_a)
        b = jnp.sqrt(-jnp.tanh(log_a) * (1.0 + a * a)) * (i * xc)
        return a.reshape(n_tiles, SUBLANES, D_RNN), b.reshape(n_tiles, SUBLANES, D_RNN)

    def fwd_chunk(c, carry):
        r0 = pl.multiple_of(c * rows, rows)
        a3, b3 = _tile_scan(*chunk_maps(r0, 0), reverse=False)
        for j in range(n_tiles):
            ht = a3[j] * carry + b3[j]
            hf[pl.ds(r0 + j * SUBLANES, SUBLANES), :] = ht
            carry = ht[SUBLANES - 1:SUBLANES, :]
        return carry

    def bwd_chunk(cc, carry):
        r0 = pl.multiple_of((n_chunks - 1 - cc) * rows, rows)
        a3, b3 = _tile_scan(*chunk_maps(r0, 1), reverse=True)
        for j in reversed(range(n_tiles)):
            ht = a3[j] * carry + b3[j]
            hb[j * SUBLANES:(j + 1) * SUBLANES, :] = ht
            carry = ht[0:1, :]
        y = (hf[pl.ds(r0, rows), :] + hb[...]) * _silu(rg_ref[pl.ds(r0, rows), :])
        y_ref[pl.ds(r0, rows), :] = y.astype(y_ref.dtype)
        return carry

    st_ref[0, 0:1, :] = lax.fori_loop(0, n_chunks, fwd_chunk, h0_ref[0, 0:1, :])
    st_ref[0, 1:2, :] = lax.fori_loop(0, n_chunks, bwd_chunk, h0_ref[0, 1:2, :])


def _rnn(za, h0, lw, layer, nb, seq):
    t = za.shape[0]
    rows = 256
    kern = functools.partial(_rnn_kernel, seq=seq, rows=rows)
    return pl.pallas_call(
        kern,
        grid=(nb,),
        in_specs=[pl.BlockSpec((seq, D_RNN), lambda b: (b, 0)),
                  pl.BlockSpec((seq, D_RNN), lambda b: (b, 1)),
                  pl.BlockSpec((1, 2, D_RNN), lambda b: (b, 0, 0)),
                  _layer_spec((CONV_W, D_RNN), layer),
                  _layer_spec((1, D_RNN), layer),
                  _layer_spec((2, D_RNN, 2 * D_RNN), layer),
                  _layer_spec((2, 1, 2 * D_RNN), layer),
                  _layer_spec((2, 1, D_RNN), layer)],
        out_specs=[pl.BlockSpec((seq, D_RNN), lambda b: (b, 0)),
                   pl.BlockSpec((1, 2, D_RNN), lambda b: (b, 0, 0))],
        out_shape=[jax.ShapeDtypeStruct((t, D_RNN), BF16),
                   jax.ShapeDtypeStruct((nb, 2, D_RNN), F32)],
        scratch_shapes=[pltpu.VMEM((seq + 2 * SUBLANES, D_RNN), F32),
                        pltpu.VMEM((seq, D_RNN), F32),
                        pltpu.VMEM((rows, D_RNN), F32)],
        compiler_params=_params("arbitrary"),
        name="rglru",
    )(za, za, h0, lw['conv_w'], lw['conv_b'], lw['wg'], lw['bg'], lw['lam'])


def _attend_t(chains, kv_refs, tk):
    blocks = [(k_ref, vt_ref, j) for k_ref, vt_ref in kv_refs for j in range(vt_ref.shape[0])]

    def scores(block):
        k_ref, _, j = block
        k_blocks = {}
        out = []
        for q, kcols, _ in chains:
            key = (kcols.start, kcols.stop)
            if key not in k_blocks:
                k_blocks[key] = k_ref[j * tk:(j + 1) * tk, kcols]
            out.append(_nt_dot(k_blocks[key], q))
        return out

    state = [None] * len(chains)
    s_cur = scores(blocks[0])
    for n, (_, vt_ref, j) in enumerate(blocks):
        s_next = scores(blocks[n + 1]) if n + 1 < len(blocks) else None
        for c, (_, _, vrows) in enumerate(chains):
            st = s_cur[c]
            m_blk = jnp.max(st, axis=0, keepdims=True)
            if state[c] is None:
                m_new = m_blk
            else:
                m, l, acc = state[c]
                m_new = jnp.maximum(m, m_blk)
                alpha = jnp.exp2(m - m_new)
            p = jnp.exp2(st - m_new)
            l_blk = jnp.sum(p, axis=0, keepdims=True)
            pv = jnp.dot(vt_ref[j, vrows, :], p.astype(BF16), preferred_element_type=F32)
            state[c] = (m_new, l_blk, pv) if state[c] is None else (m_new, alpha * l + l_blk, alpha * acc + pv)
        s_cur = s_next
    return [acc / l for _, l, acc in state]


def _kv_in_specs(kv_parts, nb, tk, k_width, vt_rows):
    specs, args = [], []
    for k, vt in kv_parts:
        n = vt.shape[0] // nb
        specs += [pl.BlockSpec((n * tk, k_width), lambda b, h, i: (b, h)),
                  pl.BlockSpec((n, vt_rows, tk), lambda b, h, i: (b, h, 0))]
        args += [k, vt]
    return specs, args


def _gate_specs(col0, width, tq, row_block):
    w = math.gcd(col0, width) if col0 else width
    return [pl.BlockSpec((tq, w), lambda b, h, i, n=n: (row_block(b, i), (col0 + h * width) // w + n))
            for n in range(width // w)]


def _gate(refs, rows):
    return jnp.concatenate([r[rows, :] for r in refs], axis=1)


def _mla_attn_kernel(q_ref, *refs, tk, n_parts, n_sub, n_heads):
    kv_refs = [(refs[2 * n], refs[2 * n + 1]) for n in range(n_parts)]
    mg_refs, o_ref = refs[2 * n_parts:-1], refs[-1]
    chains = [(q_ref[:, h * HEAD_PAD:(h + 1) * HEAD_PAD], slice(h * HEAD_PAD, (h + 1) * HEAD_PAD),
               slice(h * MLA_V, (h + 1) * MLA_V)) for h in range(n_heads)]
    ot = jnp.concatenate(_attend_t(chains, kv_refs, tk), axis=0)
    rows = slice(0, n_sub * TQ)
    o_ref[...] = (ot.T * _silu(_gate(mg_refs, rows))).astype(o_ref.dtype)


def _attn_tiling(seq, n_kv, heads):
    if n_kv == 1:
        return 1, heads
    return min(2, seq // TQ), None


def _mla_attn(q, kv_parts, zb, nb, seq, tk):
    t = q.shape[0]
    n_kv = sum(vt.shape[0] for _, vt in kv_parts) // nb
    n_sub, n_heads = _attn_tiling(seq, n_kv, MLA_H)
    n_heads = n_heads or 2
    tq = n_sub * TQ
    nq = seq // tq
    kern = functools.partial(_mla_attn_kernel, tk=tk, n_parts=len(kv_parts), n_sub=n_sub, n_heads=n_heads)
    kv_specs, kv_args = _kv_in_specs(kv_parts, nb, tk, n_heads * HEAD_PAD, n_heads * MLA_V)
    mg_specs = _gate_specs(ZB_MG, n_heads * MLA_V, tq, lambda b, i: b * nq + i)
    return pl.pallas_call(
        kern,
        grid=(nb, MLA_H // n_heads, nq),
        in_specs=[pl.BlockSpec((tq, n_heads * HEAD_PAD), lambda b, h, i: (b * nq + i, h))] + kv_specs + mg_specs,
        out_specs=pl.BlockSpec((tq, n_heads * MLA_V), lambda b, h, i: (b * nq + i, h)),
        out_shape=jax.ShapeDtypeStruct((t, MLA_H * MLA_V), BF16),
        compiler_params=_params("arbitrary", "arbitrary", "arbitrary"),
        name="mla_attn",
    )(q, *kv_args, *([zb] * len(mg_specs)))


def _diff_attn_kernel(q_ref, lam_ref, dn_ref, *refs, tk, n_parts, n_sub, n_heads, lam_init):
    kv_refs = [(refs[2 * n], refs[2 * n + 1]) for n in range(n_parts)]
    dg_refs, o_ref = refs[2 * n_parts:-1], refs[-1]
    lp = lam_ref[...]
    lam = (jnp.exp(jnp.sum(lp[0:1] * lp[1:2], axis=-1, keepdims=True))
           - jnp.exp(jnp.sum(lp[2:3] * lp[3:4], axis=-1, keepdims=True)) + lam_init)
    chains = []
    for s in range(n_sub):
        for h in range(n_heads):
            cols = slice(h * LANES, (h + 1) * LANES)
            q = q_ref[s * TQ:(s + 1) * TQ, cols].astype(F32)
            lane = lax.broadcasted_iota(jnp.int32, q.shape, 1)
            q12 = jnp.concatenate([jnp.where(lane < DIFF_DK, q, 0.0), jnp.where(lane < DIFF_DK, 0.0, q)], axis=0)
            chains.append((q12.astype(BF16), cols, cols))
    outs = _attend_t(chains, kv_refs, tk)
    for s in range(n_sub):
        rows = slice(s * TQ, (s + 1) * TQ)
        heads = []
        for h in range(n_heads):
            o12 = outs[s * n_heads + h]
            od = (o12[:, :TQ] - lam * o12[:, TQ:]).T
            heads.append(_rms(od) * dn_ref[...] * (1.0 - lam_init))
        od = heads[0] if n_heads == 1 else jnp.concatenate(heads, axis=1)
        o_ref[rows, :] = (od * _silu(_gate(dg_refs, rows))).astype(o_ref.dtype)


def _diff_attn(qd, kv_parts, zb, lw, layer, nb, seq, tk):
    t = qd.shape[0]
    n_kv = sum(vt.shape[0] for _, vt in kv_parts) // nb
    n_sub, n_heads = _attn_tiling(seq, n_kv, DIFF_H)
    n_heads = n_heads or 1
    tq = n_sub * TQ
    nq = seq // tq
    lam_init = 0.8 - 0.6 * math.exp(-0.3 * layer)
    kern = functools.partial(_diff_attn_kernel, tk=tk, n_parts=len(kv_parts), n_sub=n_sub, n_heads=n_heads,
                             lam_init=lam_init)
    width = n_heads * LANES
    kv_specs, kv_args = _kv_in_specs(kv_parts, nb, tk, width, width)
    dg_specs = _gate_specs(ZB_DG, width, tq, lambda b, i: b * nq + i)
    return pl.pallas_call(
        kern,
        grid=(nb, DIFF_H // n_heads, nq),
        in_specs=[pl.BlockSpec((tq, width), lambda b, h, i: (b * nq + i, h)),
                  _layer_spec((4, DIFF_DK), layer),
                  _layer_spec((1, DIFF_DV), layer)] + kv_specs + dg_specs,
        out_specs=pl.BlockSpec((tq, width), lambda b, h, i: (b * nq + i, h)),
        out_shape=jax.ShapeDtypeStruct((t, DIFF_COLS), BF16),
        compiler_params=_params("arbitrary", "arbitrary", "arbitrary"),
        name="diff_attn",
    )(qd, lw['lam_p'], lw['diff_norm'], *kv_args, *([zb] * len(dg_specs)))


def _merge_kernel(yr_ref, ym_ref, yd_ref, x_ref, mod_ref, gpre_ref, wg_ref, wr_ref, wm_ref, wd_ref, wo_ref,
                  gpost_ref, o_ref):
    x = x_ref[...]
    h = _modulated_norm(x, gpre_ref, mod_ref).astype(BF16)
    merged = None
    for k, (y_ref, w_ref) in enumerate(((yr_ref, wr_ref), (ym_ref, wm_ref), (yd_ref, wd_ref))):
        gate = _sigmoid(jnp.dot(h, wg_ref[:, k * D_MODEL:(k + 1) * D_MODEL], preferred_element_type=F32))
        term = gate * jnp.dot(y_ref[...], w_ref[...], preferred_element_type=F32)
        merged = term if merged is None else merged + term
    o = jnp.dot(merged.astype(BF16), wo_ref[...], preferred_element_type=F32)
    o_ref[...] = x + mod_ref[0, 2:3, :] * (_rms(o) * gpost_ref[...])


def _merge(y_rnn, y_mla, y_diff, x, mod, lw, layer, rows_per_mod):
    t, d = x.shape
    tm = 512
    tiles_per_mod = rows_per_mod // tm
    row = lambda width: pl.BlockSpec((tm, width), lambda i: (i, 0))
    return pl.pallas_call(
        _merge_kernel,
        grid=(t // tm,),
        in_specs=[row(D_RNN), row(MLA_H * MLA_V), row(DIFF_COLS), row(d),
                  pl.BlockSpec((1, 3, d), lambda i: (i // tiles_per_mod, 0, 0)),
                  _layer_spec((1, d), layer), _layer_spec((d, 3 * d), layer),
                  _layer_spec((D_RNN, d), layer), _layer_spec((MLA_H * MLA_V, d), layer),
                  _layer_spec((DIFF_COLS, d), layer), _layer_spec((d, d), layer), _layer_spec((1, d), layer)],
        out_specs=row(d),
        out_shape=jax.ShapeDtypeStruct((t, d), F32),
        compiler_params=_params("arbitrary"),
        name="merge_out",
    )(y_rnn, y_mla, y_diff, x, mod, lw['g_pre'], lw['w_mgate'], lw['w_br_rnn'], lw['w_br_mla'], lw['w_br_diff'],
      lw['w_out'], lw['g_post'])


def _pad_heads(w, width):
    l, k, _ = w.shape
    w4 = jnp.pad(w.reshape(l, k, MLA_H, width), ((0, 0), (0, 0), (0, 0), (0, HEAD_PAD - width)))
    return w4.reshape(l, k, MLA_H * HEAD_PAD)


def _w_in_kernel(w_ref, wz_ref, wg_ref):
    w = w_ref[0]
    rows = w.shape[0]
    wz_ref[0, :, 0:_O_KR] = w[:, 0:_O_KR].astype(BF16)
    wz_ref[0, :, C_KRP:C_KRP + LANES] = jnp.zeros((rows, LANES), BF16)
    wz_ref[0, :, C_KRP + MLA_NOPE:C_KRP + MLA_NOPE + MLA_ROPE] = w[:, _O_KR:_O_MG].astype(BF16)
    wz_ref[0, :, C_MG:Z_COLS] = w[:, _O_MG:_O_MGATE].astype(BF16)
    wg_ref[0] = w[:, _O_MGATE:_O_END].astype(BF16)


def _split_w_in(w_in):
    depth, d, n = w_in.shape
    tr = 256
    return pl.pallas_call(
        _w_in_kernel,
        grid=(depth, d // tr),
        in_specs=[pl.BlockSpec((1, tr, n), lambda l, i: (l, i, 0))],
        out_specs=[pl.BlockSpec((1, tr, Z_COLS), lambda l, i: (l, i, 0)),
                   pl.BlockSpec((1, tr, _O_END - _O_MGATE), lambda l, i: (l, i, 0))],
        out_shape=[jax.ShapeDtypeStruct((depth, d, Z_COLS), BF16),
                   jax.ShapeDtypeStruct((depth, d, _O_END - _O_MGATE), BF16)],
        compiler_params=_params("arbitrary", "arbitrary"),
        name="w_in_layout",
    )(w_in)


def _layer_params(p):
    depth, d, _ = p['w_in'].shape
    w_z, w_mgate = _split_w_in(p['w_in'])
    eye = jnp.eye(RNN_BLOCKS, dtype=F32)[None, None, :, None, :, None]
    dense = lambda w: (w[:, :, :, :, None, :] * eye).reshape(depth, 2, D_RNN, D_RNN)
    ukv = p['w_ukv'].reshape(depth, KV_RANK, MLA_H, MLA_NOPE + MLA_V)
    return dict(
        g_pre=p['g_pre'].reshape(depth, 1, d), g_post=p['g_post'].reshape(depth, 1, d),
        w_z=w_z, w_mgate=w_mgate,
        conv_w=p['conv_w'], conv_b=p['conv_b'].reshape(depth, 1, D_RNN),
        wg=(0.5 * jnp.concatenate([dense(p['w_rg_a']), dense(p['w_rg_x'])], axis=-1)).astype(BF16),
        bg=0.5 * jnp.concatenate([p['b_rg_a'], p['b_rg_x']], axis=-1).reshape(depth, 2, 1, 2 * D_RNN),
        lam=p['rg_lam'].reshape(depth, 2, 1, D_RNN),
        q_norm=p['q_norm'].reshape(depth, 1, Q_RANK), kv_norm=p['kv_norm'].reshape(depth, 1, KV_RANK),
        wuq=_pad_heads(p['w_uq'], MLA_NOPE + MLA_ROPE).astype(BF16),
        wuk=_pad_heads(ukv[..., :MLA_NOPE].reshape(depth, KV_RANK, MLA_H * MLA_NOPE), MLA_NOPE).astype(BF16),
        wuvt=jnp.swapaxes(ukv[..., MLA_NOPE:].reshape(depth, KV_RANK, MLA_H * MLA_V), 1, 2).astype(BF16),
        lam_p=jnp.stack([p['lam_q1'], p['lam_k1'], p['lam_q2'], p['lam_k2']], axis=1),
        diff_norm=p['diff_norm'].reshape(depth, 1, DIFF_DV),
        w_br_rnn=p['w_br_rnn'].astype(BF16), w_br_mla=p['w_br_mla'].astype(BF16),
        w_br_diff=p['w_br_diff'].astype(BF16), w_out=p['w_out'].astype(BF16))


def _rope_tables(n_pos, groups, dim):
    half = dim // 4
    inv = ROPE_THETA ** (-np.arange(half, dtype=np.float32) / half)
    inv_row = np.zeros((LANES,), np.float32)
    inv_col = np.zeros((LANES,), np.float32)
    first = np.zeros((LANES,), np.float32)
    second = np.zeros((LANES,), np.float32)
    for off in groups:
        for inv_axis, base in ((inv_row, off), (inv_col, off + 2 * half)):
            inv_axis[base:base + half] = inv
            inv_axis[base + half:base + 2 * half] = inv
            first[base:base + half] = 1.0
            second[base + half:base + 2 * half] = 1.0
    pos = jnp.arange(n_pos, dtype=jnp.int32)
    ang = ((pos // GRID_W).astype(F32)[:, None] * jnp.asarray(inv_row)[None, :]
           + (pos % GRID_W).astype(F32)[:, None] * jnp.asarray(inv_col)[None, :])
    sin = jnp.sin(ang)
    return jnp.stack([jnp.cos(ang), -sin * jnp.asarray(first), sin * jnp.asarray(second)])


def _layer(l, x, mod, lw, nb, seq, rows_per_mod, h0, tabs, ctx):
    tk = min(TK, seq)
    zs, q, ckvn, qd, (k, vt, kd, vdt) = _inproj(x, mod, tabs, lw, l, seq, rows_per_mod, tk, caches=ctx is None)
    y_rnn, st = _rnn(zs[0], h0, lw, l, nb, seq)
    mla_kv, diff_kv = [(k, vt)], [(kd, vdt)]
    if ctx is not None:
        kc, vtc, kdc, vdtc = _ctx(*ctx, lw, l, tk)
        mla_kv, diff_kv = [(kc, vtc)] + mla_kv, [(kdc, vdtc)] + diff_kv
    y_mla = _mla_attn(q, mla_kv, zs[1], nb, seq, tk)
    y_diff = _diff_attn(qd, diff_kv, zs[1], lw, l, nb, seq, tk)
    x_new = _merge(y_rnn, y_mla, y_diff, x, mod, lw, l, rows_per_mod)
    return x_new, zs[2:], ckvn, st


def kernel(x_prompt, x_sample, cache_mla_ckv, cache_mla_krope, cache_diff_k, cache_diff_v, state_rnn, c, c_ctx, w_mod, b_mod, g_pre, g_post, w_in, conv_w, conv_b, w_rg_a, b_rg_a, w_rg_x, b_rg_x, rg_lam, q_norm, w_uq, kv_norm, w_ukv, lam_q1, lam_k1, lam_q2, lam_k2, diff_norm, w_br_rnn, w_br_mla, w_br_diff, w_out):
    depth = w_in.shape[0]
    nbp, sp, d = x_prompt.shape
    nbs, ss, _ = x_sample.shape

    n_cond = -(-(1 + nbs) // SUBLANES) * SUBLANES
    cond = jnp.concatenate([c_ctx[None, :], c, jnp.zeros((n_cond - 1 - nbs, d), F32)], axis=0)
    mod = _modulation(cond, w_mod, b_mod).reshape(depth, n_cond, 3, d)

    lw = _layer_params(dict(
        w_in=w_in, g_pre=g_pre, g_post=g_post, conv_w=conv_w, conv_b=conv_b, w_rg_a=w_rg_a, b_rg_a=b_rg_a,
        w_rg_x=w_rg_x, b_rg_x=b_rg_x, rg_lam=rg_lam, q_norm=q_norm, w_uq=w_uq, kv_norm=kv_norm, w_ukv=w_ukv,
        lam_q1=lam_q1, lam_k1=lam_k1, lam_q2=lam_q2, lam_k2=lam_k2, diff_norm=diff_norm, w_br_rnn=w_br_rnn,
        w_br_mla=w_br_mla, w_br_diff=w_br_diff, w_out=w_out))

    xp = x_prompt.reshape(nbp * sp, d)
    h0 = jnp.zeros((nbp, 2, D_RNN), F32)
    ckv_l, kr_l, dk_l, dv_l, st_l = [], [], [], [], []
    for l in range(depth):
        xp, (krc, dkc, dvc), ckvn, st = _layer(l, xp, mod[l, 0:1], lw, nbp, sp, nbp * sp, h0, None, None)
        ckv_l.append(ckvn.reshape(nbp, sp, KV_RANK))
        kr_l.append(krc.reshape(nbp, sp, MLA_ROPE))
        dk_l.append(dkc.reshape(nbp, sp, DIFF_H, 2 * DIFF_DK))
        dv_l.append(dvc.reshape(nbp, sp, DIFF_H, DIFF_DV))
        st_l.append(st)

    tabs = (_rope_tables(ss, (MLA_NOPE,), MLA_ROPE), _rope_tables(ss, (0, DIFF_DK), DIFF_DK))
    ctx = (cache_mla_ckv, cache_mla_krope, cache_diff_k, cache_diff_v)
    xs = x_sample.reshape(nbs * ss, d)
    for l in range(depth):
        xs, _, _, _ = _layer(l, xs, mod[l, 1:1 + nbs], lw, nbs, ss, ss, state_rnn[:, l], tabs, ctx)

    return (xp.reshape(nbp, sp, d), xs.reshape(nbs, ss, d),
            jnp.stack(ckv_l, axis=1), jnp.stack(kr_l, axis=1), jnp.stack(dk_l, axis=1), jnp.stack(dv_l, axis=1),
            jnp.stack(st_l, axis=1))
```

```python
import functools
import math

import jax
import jax.numpy as jnp
import numpy as np
from jax import lax
from jax.experimental import pallas as pl
from jax.experimental.pallas import tpu as pltpu

F32 = jnp.float32
BF16 = jnp.bfloat16

D_MODEL = 1024
GRID_W = 64
EPS = 1e-6
ROPE_THETA = 10000.0
LOG2E = math.log2(math.e)

D_RNN = 512
RNN_BLOCKS = 8
CONV_W = 4
RG_C = 8.0

MLA_H = 8
MLA_NOPE = 64
MLA_ROPE = 32
MLA_V = 64
Q_RANK = 384
KV_RANK = 256
MLA_SCALE = (MLA_NOPE + MLA_ROPE) ** -0.5

DIFF_H = 4
DIFF_DK = 64
DIFF_DV = 2 * DIFF_DK
DIFF_SCALE = DIFF_DK ** -0.5
DIFF_COLS = DIFF_H * DIFF_DV

LANES = 128
SUBLANES = 8
HEAD_PAD = LANES
TQ = 256
TK = 512
VMEM_LIMIT = 56 * 1024 * 1024

_O_KR, _O_MG, _O_MGATE = 1664, 1696, 4256
A_RNN, A_LAT, A_COLS = 0, 1024, 1664
B_MG, B_QKV, B_DG, B_COLS = 0, 512, 2048, 2560
ZB_MG = 0
ZB_DG = 512


def _sigmoid(x):
    return 0.5 * jnp.tanh(0.5 * x) + 0.5


def _silu(x):
    return x * _sigmoid(x)


def _rms(x):
    return x * lax.rsqrt(jnp.mean(x * x, axis=-1, keepdims=True) + EPS)


def _params(*sem):
    return pltpu.CompilerParams(dimension_semantics=sem, vmem_limit_bytes=VMEM_LIMIT)


def _layer_spec(shape, layer):
    return pl.BlockSpec((None,) + tuple(shape), lambda *_: (layer,) + (0,) * len(shape))


def _mod_kernel(cond_ref, w_ref, b_ref, o_ref):
    c = cond_ref[...]
    o_ref[0] = jnp.dot(_silu(c).astype(BF16), w_ref[0].astype(BF16), preferred_element_type=F32) + b_ref[0]


def _modulation(cond, w_mod, b_mod):
    depth, d, n = w_mod.shape
    r = cond.shape[0]
    tn = 1024
    return pl.pallas_call(
        _mod_kernel,
        grid=(depth, n // tn),
        in_specs=[pl.BlockSpec((r, d), lambda l, j: (0, 0)),
                  pl.BlockSpec((1, d, tn), lambda l, j: (l, 0, j)),
                  pl.BlockSpec((1, 1, tn), lambda l, j: (l, 0, j))],
        out_specs=pl.BlockSpec((1, r, tn), lambda l, j: (l, 0, j)),
        out_shape=jax.ShapeDtypeStruct((depth, r, n), F32),
        compiler_params=_params("arbitrary", "arbitrary"),
        name="modulation",
    )(cond, w_mod, b_mod.reshape(depth, 1, n))


def _modulated_norm(x, g_ref, mod_ref):
    return (_rms(x) * g_ref[...]) * (1.0 + mod_ref[0, 1:2, :]) + mod_ref[0, 0:1, :]


def _rope(x, tab_ref, half):
    n = x.shape[-1]
    return x * tab_ref[0] + pltpu.roll(x, n - half, 1) * tab_ref[1] + pltpu.roll(x, half, 1) * tab_ref[2]


def _nt_dot(a, b):
    return lax.dot_general(a, b, (((1,), (1,)), ((), ())), preferred_element_type=F32)


def _kv_up(ckvn, krp, wuk_ref, wuvt_ref, k_ref, vt_ref):
    cb = ckvn.astype(BF16)
    kn = jnp.dot(cb, wuk_ref[...], preferred_element_type=F32)
    vt_ref[0] = _nt_dot(wuvt_ref[...], cb).astype(BF16)
    for h in range(MLA_H):
        sl = slice(h * HEAD_PAD, (h + 1) * HEAD_PAD)
        k_ref[:, sl] = (kn[:, sl] + krp).astype(BF16)


def _inproj_kernel(*refs, rope, caches):
    refs = list(refs)
    x_ref, mod_ref, g_ref, wa_ref, wkr_ref, wb_ref = refs[:6]
    tabm_ref, tabd_ref = (refs[6], refs[7]) if rope else (None, None)
    qn_ref, wuq_ref, kvn_ref, wuk_ref, wuvt_ref = refs[8:13] if rope else refs[6:11]
    outs = refs[13:] if rope else refs[11:]
    za_ref, zb_ref = outs[:2]
    krc_ref, dkc_ref, dvc_ref = outs[2:5] if caches else (None, None, None)
    q_ref, k_ref, vt_ref, ckvn_ref, qd_ref, kd_ref, vdt_ref = outs[5:] if caches else outs[2:]

    h = _modulated_norm(x_ref[...], g_ref, mod_ref).astype(BF16)
    proj = lambda w_ref, c0, c1: jnp.dot(h, w_ref[:, c0:c1], preferred_element_type=F32)
    za_ref[...] = proj(wa_ref, A_RNN, A_LAT)

    lat = proj(wa_ref, A_LAT, A_COLS)
    cqn = _rms(lat[:, 0:Q_RANK]) * qn_ref[...]
    q = jnp.dot(cqn.astype(BF16), wuq_ref[...], preferred_element_type=F32) * (MLA_SCALE * LOG2E)
    ckvn = _rms(lat[:, Q_RANK:Q_RANK + KV_RANK]) * kvn_ref[...]
    ckvn_ref[...] = ckvn
    krp = jnp.dot(h, wkr_ref[...], preferred_element_type=F32)
    if caches:
        krc_ref[...] = krp[:, MLA_NOPE:MLA_NOPE + MLA_ROPE]
    if rope:
        krp = _rope(krp, tabm_ref, MLA_ROPE // 4)
    _kv_up(ckvn, krp, wuk_ref, wuvt_ref, k_ref, vt_ref)
    for hd in range(MLA_H):
        sl = slice(hd * HEAD_PAD, (hd + 1) * HEAD_PAD)
        qh = q[:, sl]
        if rope:
            qh = _rope(qh, tabm_ref, MLA_ROPE // 4)
        q_ref[:, sl] = qh.astype(BF16)
    zb_ref[:, ZB_MG:ZB_MG + MLA_H * MLA_V] = proj(wb_ref, B_MG, B_QKV)

    qkv = proj(wb_ref, B_QKV, B_DG)
    for hd in range(DIFF_H):
        sl = slice(hd * LANES, (hd + 1) * LANES)
        qd = qkv[:, sl]
        kd = qkv[:, DIFF_COLS + hd * LANES:DIFF_COLS + (hd + 1) * LANES]
        vd = qkv[:, 2 * DIFF_COLS + hd * LANES:2 * DIFF_COLS + (hd + 1) * LANES]
        if caches:
            dkc_ref[:, hd, :] = kd
            dvc_ref[:, hd, :] = vd
        if rope:
            qd = _rope(qd, tabd_ref, DIFF_DK // 4)
            kd = _rope(kd, tabd_ref, DIFF_DK // 4)
        qd_ref[:, sl] = (qd * (DIFF_SCALE * LOG2E)).astype(BF16)
        kd_ref[:, sl] = kd.astype(BF16)
        vdt_ref[0, sl, :] = vd.T.astype(BF16)
    zb_ref[:, ZB_DG:ZB_DG + DIFF_COLS] = proj(wb_ref, B_DG, B_COLS)


def _kv_specs(tk):
    rows = lambda width: pl.BlockSpec((tk, width), lambda i: (i, 0))
    tr = lambda height: pl.BlockSpec((1, height, tk), lambda i: (i, 0, 0))
    return [rows(MLA_H * HEAD_PAD), tr(MLA_H * MLA_V), rows(DIFF_COLS), tr(DIFF_COLS)]


def _kv_shapes(n_blocks, tk):
    return [jax.ShapeDtypeStruct((n_blocks * tk, MLA_H * HEAD_PAD), BF16),
            jax.ShapeDtypeStruct((n_blocks, MLA_H * MLA_V, tk), BF16),
            jax.ShapeDtypeStruct((n_blocks * tk, DIFF_COLS), BF16),
            jax.ShapeDtypeStruct((n_blocks, DIFF_COLS, tk), BF16)]


def _inproj(x, mod, tabs, lw, layer, seq, rows_per_mod, tk, caches):
    t, d = x.shape
    tm = tk
    tiles_per_mod = rows_per_mod // tm
    rope = tabs is not None
    in_specs = [pl.BlockSpec((tm, d), lambda i: (i, 0)),
                pl.BlockSpec((1, 3, d), lambda i: (i // tiles_per_mod, 0, 0)),
                _layer_spec((1, d), layer),
                _layer_spec((d, A_COLS), layer), _layer_spec((d, LANES), layer), _layer_spec((d, B_COLS), layer)]
    args = [x, mod, lw['g_pre'], lw['w_a'], lw['w_krp'], lw['w_b']]
    if rope:
        tab_spec = pl.BlockSpec((3, tm, LANES), lambda i: (0, i % (seq // tm), 0))
        in_specs += [tab_spec, tab_spec]
        args += [tabs[0], tabs[1]]
    in_specs += [_layer_spec((1, Q_RANK), layer), _layer_spec((Q_RANK, MLA_H * HEAD_PAD), layer),
                 _layer_spec((1, KV_RANK), layer), _layer_spec((KV_RANK, MLA_H * HEAD_PAD), layer),
                 _layer_spec((MLA_H * MLA_V, KV_RANK), layer)]
    args += [lw['q_norm'], lw['wuq'], lw['kv_norm'], lw['wuk'], lw['wuvt']]
    row = lambda width: pl.BlockSpec((tm, width), lambda i: (i, 0))
    f32 = lambda width: jax.ShapeDtypeStruct((t, width), F32)
    bf16 = lambda width: jax.ShapeDtypeStruct((t, width), BF16)
    k_spec, vt_spec, kd_spec, vdt_spec = _kv_specs(tk)
    k_shape, vt_shape, kd_shape, vdt_shape = _kv_shapes(t // tk, tk)
    qk = MLA_H * HEAD_PAD
    z_specs = [row(A_LAT), row(ZB_DG + DIFF_COLS)]
    z_shapes = [f32(A_LAT), f32(ZB_DG + DIFF_COLS)]
    if caches:
        heads = pl.BlockSpec((tm, DIFF_H, DIFF_DV), lambda i: (i, 0, 0))
        z_specs += [row(MLA_ROPE), heads, heads]
        z_shapes += [f32(MLA_ROPE)] + [jax.ShapeDtypeStruct((t, DIFF_H, DIFF_DV), F32)] * 2
    outs = pl.pallas_call(
        functools.partial(_inproj_kernel, rope=rope, caches=caches),
        grid=(t // tm,),
        in_specs=in_specs,
        out_specs=z_specs + [row(qk), k_spec, vt_spec, row(KV_RANK), row(DIFF_COLS), kd_spec, vdt_spec],
        out_shape=z_shapes + [bf16(qk), k_shape, vt_shape, f32(KV_RANK), bf16(DIFF_COLS), kd_shape, vdt_shape],
        compiler_params=_params("arbitrary"),
        name="inproj",
    )(*args)
    zs, (q, k, vt, ckvn, qd, kd, vdt) = outs[:len(z_specs)], outs[len(z_specs):]
    return zs, q, ckvn, qd, (k, vt, kd, vdt)


def _ctx_kernel(ckvn_ref, kr_ref, dk_ref, dv_ref, wuk_ref, wuvt_ref, k_ref, vt_ref, kd_ref, vdt_ref):
    row = lax.broadcasted_iota(jnp.int32, (MLA_ROPE, LANES), 0)
    lane = lax.broadcasted_iota(jnp.int32, (MLA_ROPE, LANES), 1)
    place = (lane == row + MLA_NOPE).astype(BF16)
    krp = jnp.dot(kr_ref[...].astype(BF16), place, preferred_element_type=F32)
    _kv_up(ckvn_ref[...], krp, wuk_ref, wuvt_ref, k_ref, vt_ref)
    for h in range(DIFF_H):
        sl = slice(h * LANES, (h + 1) * LANES)
        kd_ref[:, sl] = dk_ref[:, h, :].astype(BF16)
        vdt_ref[0, sl, :] = dv_ref[:, h, :].T.astype(BF16)


def _ctx(ckvn, kr, dk, dv, lw, layer, tk):
    nb, _, past, _ = ckvn.shape
    n_past = past // tk
    cached = lambda *minor: pl.BlockSpec((None, None, tk) + minor,
                                         lambda i: (i // n_past, layer, i % n_past) + (0,) * len(minor))
    return pl.pallas_call(
        _ctx_kernel,
        grid=(nb * n_past,),
        in_specs=[cached(KV_RANK), cached(MLA_ROPE), cached(DIFF_H, DIFF_DV), cached(DIFF_H, DIFF_DV),
                  _layer_spec((KV_RANK, MLA_H * HEAD_PAD), layer), _layer_spec((MLA_H * MLA_V, KV_RANK), layer)],
        out_specs=_kv_specs(tk),
        out_shape=_kv_shapes(nb * n_past, tk),
        compiler_params=_params("arbitrary"),
        name="ctx_kv",
    )(ckvn, kr, dk, dv, lw['wuk'], lw['wuvt'])


def _tile_scan(a, b, reverse):
    r_in_tile = lax.broadcasted_iota(jnp.int32, a.shape, 1)
    for k in (1, 2, 4):
        shift = SUBLANES - k if reverse else k
        valid = (r_in_tile < SUBLANES - k) if reverse else (r_in_tile >= k)
        a_s = jnp.where(valid, pltpu.roll(a, shift, 1), 1.0)
        b_s = jnp.where(valid, pltpu.roll(b, shift, 1), 0.0)
        b = a * b_s + b
        a = a * a_s
    return a, b


def _rnn_kernel(rx_ref, rg_ref, h0_ref, cw_ref, cb_ref, wg_ref, bg_ref, lam_ref, y_ref, st_ref,
                xpad, hf, hb, *, seq, rows):
    n_chunks = seq // rows
    n_tiles = rows // SUBLANES
    pad = SUBLANES

    xpad[0:pad, :] = jnp.zeros((pad, D_RNN), F32)
    xpad[seq + pad:seq + 2 * pad, :] = jnp.zeros((pad, D_RNN), F32)

    def copy_chunk(c, carry):
        r0 = pl.multiple_of(c * rows, rows)
        xpad[pl.ds(r0 + pad, rows), :] = rx_ref[pl.ds(r0, rows), :]
        return carry

    lax.fori_loop(0, n_chunks, copy_chunk, 0)

    def chunk_maps(r0, d):
        n = rows + 2 * pad
        xe = xpad[pl.ds(r0, n), :]
        taps = (pltpu.roll(xe, 2, 0), pltpu.roll(xe, 1, 0), xe, pltpu.roll(xe, n - 1, 0))
        xc = cb_ref[...]
        for k in range(CONV_W):
            xc = xc + taps[k][pad:pad + rows] * cw_ref[k:k + 1, :]
        t = jnp.tanh(jnp.dot(xc.astype(BF16), wg_ref[d], preferred_element_type=F32) + bg_ref[d])
        i = 0.5 * t[:, D_RNN:] + 0.5
        neg_lam = -lam_ref[d]
        softplus = jnp.maximum(neg_lam, 0.0) + jnp.log1p(jnp.exp(-jnp.abs(neg_lam)))
        c = (-0.5 * RG_C) * softplus
        log_a = c * t[:, :D_RNN] + c
        a = jnp.exp(log_a)
        b = jnp.sqrt(-jnp.tanh(log_a) * (1.0 + a * a)) * (i * xc)
        return a.reshape(n_tiles, SUBLANES, D_RNN), b.reshape(n_tiles, SUBLANES, D_RNN)

    def fwd_chunk(c, carry):
        r0 = pl.multiple_of(c * rows, rows)
        a3, b3 = _tile_scan(*chunk_maps(r0, 0), reverse=False)
        for j in range(n_tiles):
            ht = a3[j] * carry + b3[j]
            hf[pl.ds(r0 + j * SUBLANES, SUBLANES), :] = ht
            carry = ht[SUBLANES - 1:SUBLANES, :]
        return carry

    def bwd_chunk(cc, carry):
        r0 = pl.multiple_of((n_chunks - 1 - cc) * rows, rows)
        a3, b3 = _tile_scan(*chunk_maps(r0, 1), reverse=True)
        for j in reversed(range(n_tiles)):
            ht = a3[j] * carry + b3[j]
            hb[j * SUBLANES:(j + 1) * SUBLANES, :] = ht
            carry = ht[0:1, :]
        y = (hf[pl.ds(r0, rows), :] + hb[...]) * _silu(rg_ref[pl.ds(r0, rows), :])
        y_ref[pl.ds(r0, rows), :] = y.astype(y_ref.dtype)
        return carry

    st_ref[0, 0:1, :] = lax.fori_loop(0, n_chunks, fwd_chunk, h0_ref[0, 0:1, :])
    st_ref[0, 1:2, :] = lax.fori_loop(0, n_chunks, bwd_chunk, h0_ref[0, 1:2, :])


def _rnn(za, h0, lw, layer, nb, seq):
    t = za.shape[0]
    rows = 256
    kern = functools.partial(_rnn_kernel, seq=seq, rows=rows)
    return pl.pallas_call(
        kern,
        grid=(nb,),
        in_specs=[pl.BlockSpec((seq, D_RNN), lambda b: (b, 0)),
                  pl.BlockSpec((seq, D_RNN), lambda b: (b, 1)),
                  pl.BlockSpec((1, 2, D_RNN), lambda b: (b, 0, 0)),
                  _layer_spec((CONV_W, D_RNN), layer),
                  _layer_spec((1, D_RNN), layer),
                  _layer_spec((2, D_RNN, 2 * D_RNN), layer),
                  _layer_spec((2, 1, 2 * D_RNN), layer),
                  _layer_spec((2, 1, D_RNN), layer)],
        out_specs=[pl.BlockSpec((seq, D_RNN), lambda b: (b, 0)),
                   pl.BlockSpec((1, 2, D_RNN), lambda b: (b, 0, 0))],
        out_shape=[jax.ShapeDtypeStruct((t, D_RNN), BF16),
                   jax.ShapeDtypeStruct((nb, 2, D_RNN), F32)],
        scratch_shapes=[pltpu.VMEM((seq + 2 * SUBLANES, D_RNN), F32),
                        pltpu.VMEM((seq, D_RNN), F32),
                        pltpu.VMEM((rows, D_RNN), F32)],
        compiler_params=_params("arbitrary"),
        name="rglru",
    )(za, za, h0, lw['conv_w'], lw['conv_b'], lw['wg'], lw['bg'], lw['lam'])


def _attend_t(chains, kv_refs, tk):
    blocks = [(k_ref, vt_ref, j) for k_ref, vt_ref in kv_refs for j in range(vt_ref.shape[0])]

    def scores(block):
        k_ref, _, j = block
        k_blocks = {}
        out = []
        for q, kcols, _ in chains:
            key = (kcols.start, kcols.stop)
            if key not in k_blocks:
                k_blocks[key] = k_ref[j * tk:(j + 1) * tk, kcols]
            out.append(_nt_dot(k_blocks[key], q))
        return out

    state = [None] * len(chains)
    s_cur = scores(blocks[0])
    for n, (_, vt_ref, j) in enumerate(blocks):
        s_next = scores(blocks[n + 1]) if n + 1 < len(blocks) else None
        for c, (_, _, vrows) in enumerate(chains):
            st = s_cur[c]
            m_blk = jnp.max(st, axis=0, keepdims=True)
            if state[c] is None:
                m_new = m_blk
            else:
                m, l, acc = state[c]
                m_new = jnp.maximum(m, m_blk)
                alpha = jnp.exp2(m - m_new)
            p = jnp.exp2(st - m_new)
            l_blk = jnp.sum(p, axis=0, keepdims=True)
            pv = jnp.dot(vt_ref[j, vrows, :], p.astype(BF16), preferred_element_type=F32)
            state[c] = (m_new, l_blk, pv) if state[c] is None else (m_new, alpha * l + l_blk, alpha * acc + pv)
        s_cur = s_next
    return [acc / l for _, l, acc in state]


def _kv_in_specs(kv_parts, nb, tk, k_width, vt_rows):
    specs, args = [], []
    for k, vt in kv_parts:
        n = vt.shape[0] // nb
        specs += [pl.BlockSpec((n * tk, k_width), lambda b, h, i: (b, h)),
                  pl.BlockSpec((n, vt_rows, tk), lambda b, h, i: (b, h, 0))]
        args += [k, vt]
    return specs, args


def _gate_specs(col0, width, tq, row_block):
    w = math.gcd(col0, width) if col0 else width
    return [pl.BlockSpec((tq, w), lambda b, h, i, n=n: (row_block(b, i), (col0 + h * width) // w + n))
            for n in range(width // w)]


def _gate(refs, rows):
    return jnp.concatenate([r[rows, :] for r in refs], axis=1)


def _mla_attn_kernel(q_ref, *refs, tk, n_parts, n_sub, n_heads):
    kv_refs = [(refs[2 * n], refs[2 * n + 1]) for n in range(n_parts)]
    mg_refs, o_ref = refs[2 * n_parts:-1], refs[-1]
    chains = [(q_ref[:, h * HEAD_PAD:(h + 1) * HEAD_PAD], slice(h * HEAD_PAD, (h + 1) * HEAD_PAD),
               slice(h * MLA_V, (h + 1) * MLA_V)) for h in range(n_heads)]
    ot = jnp.concatenate(_attend_t(chains, kv_refs, tk), axis=0)
    rows = slice(0, n_sub * TQ)
    o_ref[...] = (ot.T * _silu(_gate(mg_refs, rows))).astype(o_ref.dtype)


def _attn_tiling(seq, n_kv, heads):
    if n_kv == 1:
        return 1, heads
    return min(4, seq // TQ), None


def _mla_attn(q, kv_parts, zb, nb, seq, tk):
    t = q.shape[0]
    n_kv = sum(vt.shape[0] for _, vt in kv_parts) // nb
    n_sub, n_heads = _attn_tiling(seq, n_kv, MLA_H)
    n_heads = n_heads or 2
    tq = n_sub * TQ
    nq = seq // tq
    kern = functools.partial(_mla_attn_kernel, tk=tk, n_parts=len(kv_parts), n_sub=n_sub, n_heads=n_heads)
    kv_specs, kv_args = _kv_in_specs(kv_parts, nb, tk, n_heads * HEAD_PAD, n_heads * MLA_V)
    mg_specs = _gate_specs(ZB_MG, n_heads * MLA_V, tq, lambda b, i: b * nq + i)
    return pl.pallas_call(
        kern,
        grid=(nb, MLA_H // n_heads, nq),
        in_specs=[pl.BlockSpec((tq, n_heads * HEAD_PAD), lambda b, h, i: (b * nq + i, h))] + kv_specs + mg_specs,
        out_specs=pl.BlockSpec((tq, n_heads * MLA_V), lambda b, h, i: (b * nq + i, h)),
        out_shape=jax.ShapeDtypeStruct((t, MLA_H * MLA_V), BF16),
        compiler_params=_params("arbitrary", "arbitrary", "arbitrary"),
        name="mla_attn",
    )(q, *kv_args, *([zb] * len(mg_specs)))


def _diff_attn_kernel(q_ref, lam_ref, dn_ref, *refs, tk, n_parts, n_sub, n_heads, lam_init):
    kv_refs = [(refs[2 * n], refs[2 * n + 1]) for n in range(n_parts)]
    dg_refs, o_ref = refs[2 * n_parts:-1], refs[-1]
    lp = lam_ref[...]
    lam = (jnp.exp(jnp.sum(lp[0:1] * lp[1:2], axis=-1, keepdims=True))
           - jnp.exp(jnp.sum(lp[2:3] * lp[3:4], axis=-1, keepdims=True)) + lam_init)
    chains = []
    for s in range(n_sub):
        for h in range(n_heads):
            cols = slice(h * LANES, (h + 1) * LANES)
            q = q_ref[s * TQ:(s + 1) * TQ, cols].astype(F32)
            lane = lax.broadcasted_iota(jnp.int32, q.shape, 1)
            q12 = jnp.concatenate([jnp.where(lane < DIFF_DK, q, 0.0), jnp.where(lane < DIFF_DK, 0.0, q)], axis=0)
            chains.append((q12.astype(BF16), cols, cols))
    outs = _attend_t(chains, kv_refs, tk)
    for s in range(n_sub):
        rows = slice(s * TQ, (s + 1) * TQ)
        heads = []
        for h in range(n_heads):
            o12 = outs[s * n_heads + h]
            od = (o12[:, :TQ] - lam * o12[:, TQ:]).T
            heads.append(_rms(od) * dn_ref[...] * (1.0 - lam_init))
        od = heads[0] if n_heads == 1 else jnp.concatenate(heads, axis=1)
        o_ref[rows, :] = (od * _silu(_gate(dg_refs, rows))).astype(o_ref.dtype)


def _diff_attn(qd, kv_parts, zb, lw, layer, nb, seq, tk):
    t = qd.shape[0]
    n_kv = sum(vt.shape[0] for _, vt in kv_parts) // nb
    n_sub, n_heads = _attn_tiling(seq, n_kv, DIFF_H)
    n_heads = n_heads or 1
    tq = n_sub * TQ
    nq = seq // tq
    lam_init = 0.8 - 0.6 * math.exp(-0.3 * layer)
    kern = functools.partial(_diff_attn_kernel, tk=tk, n_parts=len(kv_parts), n_sub=n_sub, n_heads=n_heads,
                             lam_init=lam_init)
    width = n_heads * LANES
    kv_specs, kv_args = _kv_in_specs(kv_parts, nb, tk, width, width)
    dg_specs = _gate_specs(ZB_DG, width, tq, lambda b, i: b * nq + i)
    return pl.pallas_call(
        kern,
        grid=(nb, DIFF_H // n_heads, nq),
        in_specs=[pl.BlockSpec((tq, width), lambda b, h, i: (b * nq + i, h)),
                  _layer_spec((4, DIFF_DK), layer),
                  _layer_spec((1, DIFF_DV), layer)] + kv_specs + dg_specs,
        out_specs=pl.BlockSpec((tq, width), lambda b, h, i: (b * nq + i, h)),
        out_shape=jax.ShapeDtypeStruct((t, DIFF_COLS), BF16),
        compiler_params=_params("arbitrary", "arbitrary", "arbitrary"),
        name="diff_attn",
    )(qd, lw['lam_p'], lw['diff_norm'], *kv_args, *([zb] * len(dg_specs)))


def _merge_kernel(yr_ref, ym_ref, yd_ref, x_ref, mod_ref, gpre_ref, wg_ref, wr_ref, wm_ref, wd_ref, wo_ref,
                  gpost_ref, o_ref):
    x = x_ref[...]
    h = _modulated_norm(x, gpre_ref, mod_ref).astype(BF16)
    merged = None
    for k, (y_ref, w_ref) in enumerate(((yr_ref, wr_ref), (ym_ref, wm_ref), (yd_ref, wd_ref))):
        gate = _sigmoid(jnp.dot(h, wg_ref[:, k * D_MODEL:(k + 1) * D_MODEL], preferred_element_type=F32))
        term = gate * jnp.dot(y_ref[...], w_ref[...], preferred_element_type=F32)
        merged = term if merged is None else merged + term
    o = jnp.dot(merged.astype(BF16), wo_ref[...], preferred_element_type=F32)
    o_ref[...] = x + mod_ref[0, 2:3, :] * (_rms(o) * gpost_ref[...])


def _merge(y_rnn, y_mla, y_diff, x, mod, lw, layer, rows_per_mod):
    t, d = x.shape
    tm = 512
    tiles_per_mod = rows_per_mod // tm
    row = lambda width: pl.BlockSpec((tm, width), lambda i: (i, 0))
    return pl.pallas_call(
        _merge_kernel,
        grid=(t // tm,),
        in_specs=[row(D_RNN), row(MLA_H * MLA_V), row(DIFF_COLS), row(d),
                  pl.BlockSpec((1, 3, d), lambda i: (i // tiles_per_mod, 0, 0)),
                  _layer_spec((1, d), layer), _layer_spec((d, 3 * d), layer),
                  _layer_spec((D_RNN, d), layer), _layer_spec((MLA_H * MLA_V, d), layer),
                  _layer_spec((DIFF_COLS, d), layer), _layer_spec((d, d), layer), _layer_spec((1, d), layer)],
        out_specs=row(d),
        out_shape=jax.ShapeDtypeStruct((t, d), F32),
        compiler_params=_params("arbitrary"),
        name="merge_out",
    )(y_rnn, y_mla, y_diff, x, mod, lw['g_pre'], lw['w_mgate'], lw['w_br_rnn'], lw['w_br_mla'], lw['w_br_diff'],
      lw['w_out'], lw['g_post'])


def _pad_heads(w, width):
    l, k, _ = w.shape
    w4 = jnp.pad(w.reshape(l, k, MLA_H, width), ((0, 0), (0, 0), (0, 0), (0, HEAD_PAD - width)))
    return w4.reshape(l, k, MLA_H * HEAD_PAD)


def _layer_params(p):
    depth, d, _ = p['w_in'].shape
    w_in = p['w_in']
    kr_pad = ((0, 0), (0, 0), (MLA_NOPE, HEAD_PAD - MLA_NOPE - MLA_ROPE))
    eye = jnp.eye(RNN_BLOCKS, dtype=F32)[None, None, :, None, :, None]
    dense = lambda w: (w[:, :, :, :, None, :] * eye).reshape(depth, 2, D_RNN, D_RNN)
    ukv = p['w_ukv'].reshape(depth, KV_RANK, MLA_H, MLA_NOPE + MLA_V)
    return dict(
        g_pre=p['g_pre'].reshape(depth, 1, d), g_post=p['g_post'].reshape(depth, 1, d),
        w_a=w_in[:, :, :_O_KR].astype(BF16), w_krp=jnp.pad(w_in[:, :, _O_KR:_O_MG], kr_pad).astype(BF16),
        w_b=w_in[:, :, _O_MG:_O_MGATE].astype(BF16), w_mgate=w_in[:, :, _O_MGATE:].astype(BF16),
        conv_w=p['conv_w'], conv_b=p['conv_b'].reshape(depth, 1, D_RNN),
        wg=(0.5 * jnp.concatenate([dense(p['w_rg_a']), dense(p['w_rg_x'])], axis=-1)).astype(BF16),
        bg=0.5 * jnp.concatenate([p['b_rg_a'], p['b_rg_x']], axis=-1).reshape(depth, 2, 1, 2 * D_RNN),
        lam=p['rg_lam'].reshape(depth, 2, 1, D_RNN),
        q_norm=p['q_norm'].reshape(depth, 1, Q_RANK), kv_norm=p['kv_norm'].reshape(depth, 1, KV_RANK),
        wuq=_pad_heads(p['w_uq'], MLA_NOPE + MLA_ROPE).astype(BF16),
        wuk=_pad_heads(ukv[..., :MLA_NOPE].reshape(depth, KV_RANK, MLA_H * MLA_NOPE), MLA_NOPE).astype(BF16),
        wuvt=jnp.swapaxes(ukv[..., MLA_NOPE:].reshape(depth, KV_RANK, MLA_H * MLA_V), 1, 2).astype(BF16),
        lam_p=jnp.stack([p['lam_q1'], p['lam_k1'], p['lam_q2'], p['lam_k2']], axis=1),
        diff_norm=p['diff_norm'].reshape(depth, 1, DIFF_DV),
        w_br_rnn=p['w_br_rnn'].astype(BF16), w_br_mla=p['w_br_mla'].astype(BF16),
        w_br_diff=p['w_br_diff'].astype(BF16), w_out=p['w_out'].astype(BF16))


def _rope_tables(n_pos, groups, dim):
    half = dim // 4
    inv = ROPE_THETA ** (-np.arange(half, dtype=np.float32) / half)
    inv_row = np.zeros((LANES,), np.float32)
    inv_col = np.zeros((LANES,), np.float32)
    first = np.zeros((LANES,), np.float32)
    second = np.zeros((LANES,), np.float32)
    for off in groups:
        for inv_axis, base in ((inv_row, off), (inv_col, off + 2 * half)):
            inv_axis[base:base + half] = inv
            inv_axis[base + half:base + 2 * half] = inv
            first[base:base + half] = 1.0
            second[base + half:base + 2 * half] = 1.0
    pos = jnp.arange(n_pos, dtype=jnp.int32)
    ang = ((pos // GRID_W).astype(F32)[:, None] * jnp.asarray(inv_row)[None, :]
           + (pos % GRID_W).astype(F32)[:, None] * jnp.asarray(inv_col)[None, :])
    sin = jnp.sin(ang)
    return jnp.stack([jnp.cos(ang), -sin * jnp.asarray(first), sin * jnp.asarray(second)])


def _layer(l, x, mod, lw, nb, seq, rows_per_mod, h0, tabs, ctx):
    tk = min(TK, seq)
    zs, q, ckvn, qd, (k, vt, kd, vdt) = _inproj(x, mod, tabs, lw, l, seq, rows_per_mod, tk, caches=ctx is None)
    y_rnn, st = _rnn(zs[0], h0, lw, l, nb, seq)
    mla_kv, diff_kv = [(k, vt)], [(kd, vdt)]
    if ctx is not None:
        kc, vtc, kdc, vdtc = _ctx(*ctx, lw, l, tk)
        mla_kv, diff_kv = [(kc, vtc)] + mla_kv, [(kdc, vdtc)] + diff_kv
    y_mla = _mla_attn(q, mla_kv, zs[1], nb, seq, tk)
    y_diff = _diff_attn(qd, diff_kv, zs[1], lw, l, nb, seq, tk)
    x_new = _merge(y_rnn, y_mla, y_diff, x, mod, lw, l, rows_per_mod)
    return x_new, zs[2:], ckvn, st


def kernel(x_prompt, x_sample, cache_mla_ckv, cache_mla_krope, cache_diff_k, cache_diff_v, state_rnn, c, c_ctx, w_mod, b_mod, g_pre, g_post, w_in, conv_w, conv_b, w_rg_a, b_rg_a, w_rg_x, b_rg_x, rg_lam, q_norm, w_uq, kv_norm, w_ukv, lam_q1, lam_k1, lam_q2, lam_k2, diff_norm, w_br_rnn, w_br_mla, w_br_diff, w_out):
    depth = w_in.shape[0]
    nbp, sp, d = x_prompt.shape
    nbs, ss, _ = x_sample.shape

    n_cond = -(-(1 + nbs) // SUBLANES) * SUBLANES
    cond = jnp.concatenate([c_ctx[None, :], c, jnp.zeros((n_cond - 1 - nbs, d), F32)], axis=0)
    mod = _modulation(cond, w_mod, b_mod).reshape(depth, n_cond, 3, d)

    lw = _layer_params(dict(
        w_in=w_in, g_pre=g_pre, g_post=g_post, conv_w=conv_w, conv_b=conv_b, w_rg_a=w_rg_a, b_rg_a=b_rg_a,
        w_rg_x=w_rg_x, b_rg_x=b_rg_x, rg_lam=rg_lam, q_norm=q_norm, w_uq=w_uq, kv_norm=kv_norm, w_ukv=w_ukv,
        lam_q1=lam_q1, lam_k1=lam_k1, lam_q2=lam_q2, lam_k2=lam_k2, diff_norm=diff_norm, w_br_rnn=w_br_rnn,
        w_br_mla=w_br_mla, w_br_diff=w_br_diff, w_out=w_out))

    xp = x_prompt.reshape(nbp * sp, d)
    h0 = jnp.zeros((nbp, 2, D_RNN), F32)
    ckv_l, kr_l, dk_l, dv_l, st_l = [], [], [], [], []
    for l in range(depth):
        xp, (krc, dkc, dvc), ckvn, st = _layer(l, xp, mod[l, 0:1], lw, nbp, sp, nbp * sp, h0, None, None)
        ckv_l.append(ckvn.reshape(nbp, sp, KV_RANK))
        kr_l.append(krc.reshape(nbp, sp, MLA_ROPE))
        dk_l.append(dkc.reshape(nbp, sp, DIFF_H, 2 * DIFF_DK))
        dv_l.append(dvc.reshape(nbp, sp, DIFF_H, DIFF_DV))
        st_l.append(st)

    tabs = (_rope_tables(ss, (MLA_NOPE,), MLA_ROPE), _rope_tables(ss, (0, DIFF_DK), DIFF_DK))
    ctx = (cache_mla_ckv, cache_mla_krope, cache_diff_k, cache_diff_v)
    xs = x_sample.reshape(nbs * ss, d)
    for l in range(depth):
        xs, _, _, _ = _layer(l, xs, mod[l, 1:1 + nbs], lw, nbs, ss, ss, state_rnn[:, l], tabs, ctx)

    return (xp.reshape(nbp, sp, d), xs.reshape(nbs, ss, d),
            jnp.stack(ckv_l, axis=1), jnp.stack(kr_l, axis=1), jnp.stack(dk_l, axis=1), jnp.stack(dv_l, axis=1),
            jnp.stack(st_l, axis=1))
```

```python
import functools
import math

import jax
import jax.numpy as jnp
import numpy as np
from jax import lax
from jax.experimental import pallas as pl
from jax.experimental.pallas import tpu as pltpu

F32 = jnp.float32
BF16 = jnp.bfloat16

D_MODEL = 1024
GRID_W = 64
EPS = 1e-6
ROPE_THETA = 10000.0
LOG2E = math.log2(math.e)

D_RNN = 512
RNN_BLOCKS = 8
CONV_W = 4
RG_C = 8.0

MLA_H = 8
MLA_NOPE = 64
MLA_ROPE = 32
MLA_V = 64
Q_RANK = 384
KV_RANK = 256
MLA_SCALE = (MLA_NOPE + MLA_ROPE) ** -0.5

DIFF_H = 4
DIFF_DK = 64
DIFF_DV = 2 * DIFF_DK
DIFF_SCALE = DIFF_DK ** -0.5
DIFF_COLS = DIFF_H * DIFF_DV

LANES = 128
SUBLANES = 8
HEAD_PAD = LANES
TQ = 256
TK = 512
VMEM_LIMIT = 56 * 1024 * 1024

_O_KR, _O_MG, _O_MGATE = 1664, 1696, 4256
A_RNN, A_LAT, A_COLS = 0, 1024, 1664
B_MG, B_QKV, B_DG, B_COLS = 0, 512, 2048, 2560
ZB_MG = 0
ZB_DG = 512


def _sigmoid(x):
    return 0.5 * jnp.tanh(0.5 * x) + 0.5


def _silu(x):
    return x * _sigmoid(x)


def _rms(x):
    return x * lax.rsqrt(jnp.mean(x * x, axis=-1, keepdims=True) + EPS)


def _params(*sem):
    return pltpu.CompilerParams(dimension_semantics=sem, vmem_limit_bytes=VMEM_LIMIT)


def _layer_spec(shape, layer):
    return pl.BlockSpec((None,) + tuple(shape), lambda *_: (layer,) + (0,) * len(shape))


def _mod_kernel(cond_ref, w_ref, b_ref, o_ref):
    c = cond_ref[...]
    o_ref[0] = jnp.dot(_silu(c).astype(BF16), w_ref[0].astype(BF16), preferred_element_type=F32) + b_ref[0]


def _modulation(cond, w_mod, b_mod):
    depth, d, n = w_mod.shape
    r = cond.shape[0]
    tn = 1024
    return pl.pallas_call(
        _mod_kernel,
        grid=(depth, n // tn),
        in_specs=[pl.BlockSpec((r, d), lambda l, j: (0, 0)),
                  pl.BlockSpec((1, d, tn), lambda l, j: (l, 0, j)),
                  pl.BlockSpec((1, 1, tn), lambda l, j: (l, 0, j))],
        out_specs=pl.BlockSpec((1, r, tn), lambda l, j: (l, 0, j)),
        out_shape=jax.ShapeDtypeStruct((depth, r, n), F32),
        compiler_params=_params("arbitrary", "arbitrary"),
        name="modulation",
    )(cond, w_mod, b_mod.reshape(depth, 1, n))


def _modulated_norm(x, g_ref, mod_ref):
    return (_rms(x) * g_ref[...]) * (1.0 + mod_ref[0, 1:2, :]) + mod_ref[0, 0:1, :]


def _rope(x, tab_ref, half):
    n = x.shape[-1]
    return x * tab_ref[0] + pltpu.roll(x, n - half, 1) * tab_ref[1] + pltpu.roll(x, half, 1) * tab_ref[2]


def _nt_dot(a, b):
    return lax.dot_general(a, b, (((1,), (1,)), ((), ())), preferred_element_type=F32)


def _kv_up(ckvn, krp, wuk_ref, wuvt_ref, k_ref, vt_ref):
    cb = ckvn.astype(BF16)
    kn = jnp.dot(cb, wuk_ref[...], preferred_element_type=F32)
    vt_ref[0] = _nt_dot(wuvt_ref[...], cb).astype(BF16)
    for h in range(MLA_H):
        sl = slice(h * HEAD_PAD, (h + 1) * HEAD_PAD)
        k_ref[:, sl] = (kn[:, sl] + krp).astype(BF16)


def _inproj_kernel(*refs, rope, caches, n_prev):
    refs = list(refs)
    x_ref, mod_ref, g_ref, wa_ref, wkr_ref, wb_ref = refs[:6]
    tabm_ref, tabd_ref = (refs[6], refs[7]) if rope else (None, None)
    n_in = (13 if rope else 11) + 4 * n_prev
    qn_ref, wuq_ref, kvn_ref, wuk_ref, wuvt_ref = refs[n_in - 4 * n_prev - 5:n_in - 4 * n_prev]
    prev = refs[n_in - 4 * n_prev:n_in]
    outs = refs[n_in:]
    za_ref, zb_ref = outs[:2]
    krc_ref, dkc_ref, dvc_ref = outs[2:5] if caches else (None, None, None)
    q_ref, k_ref, vt_ref, ckvn_ref, qd_ref, kd_ref, vdt_ref = outs[5:] if caches else outs[2:]
    if n_prev:
        for l in range(n_prev):
            for dst, src in zip((krc_ref, dkc_ref, dvc_ref, ckvn_ref), prev[4 * l:4 * l + 4]):
                dst[0, l] = src[...]
        krc_ref, dkc_ref, dvc_ref, ckvn_ref = (r.at[0, n_prev] for r in (krc_ref, dkc_ref, dvc_ref, ckvn_ref))

    h = _modulated_norm(x_ref[...], g_ref, mod_ref).astype(BF16)
    proj = lambda w_ref, c0, c1: jnp.dot(h, w_ref[:, c0:c1], preferred_element_type=F32)
    za_ref[...] = proj(wa_ref, A_RNN, A_LAT)

    lat = proj(wa_ref, A_LAT, A_COLS)
    cqn = _rms(lat[:, 0:Q_RANK]) * qn_ref[...]
    q = jnp.dot(cqn.astype(BF16), wuq_ref[...], preferred_element_type=F32) * (MLA_SCALE * LOG2E)
    ckvn = _rms(lat[:, Q_RANK:Q_RANK + KV_RANK]) * kvn_ref[...]
    ckvn_ref[...] = ckvn
    krp = jnp.dot(h, wkr_ref[...], preferred_element_type=F32)
    if caches:
        krc_ref[...] = krp[:, MLA_NOPE:MLA_NOPE + MLA_ROPE]
    if rope:
        krp = _rope(krp, tabm_ref, MLA_ROPE // 4)
    _kv_up(ckvn, krp, wuk_ref, wuvt_ref, k_ref, vt_ref)
    for hd in range(MLA_H):
        sl = slice(hd * HEAD_PAD, (hd + 1) * HEAD_PAD)
        qh = q[:, sl]
        if rope:
            qh = _rope(qh, tabm_ref, MLA_ROPE // 4)
        q_ref[:, sl] = qh.astype(BF16)
    zb_ref[:, ZB_MG:ZB_MG + MLA_H * MLA_V] = proj(wb_ref, B_MG, B_QKV)

    qkv = proj(wb_ref, B_QKV, B_DG)
    for hd in range(DIFF_H):
        sl = slice(hd * LANES, (hd + 1) * LANES)
        qd = qkv[:, sl]
        kd = qkv[:, DIFF_COLS + hd * LANES:DIFF_COLS + (hd + 1) * LANES]
        vd = qkv[:, 2 * DIFF_COLS + hd * LANES:2 * DIFF_COLS + (hd + 1) * LANES]
        if caches:
            dkc_ref[:, hd, :] = kd
            dvc_ref[:, hd, :] = vd
        if rope:
            qd = _rope(qd, tabd_ref, DIFF_DK // 4)
            kd = _rope(kd, tabd_ref, DIFF_DK // 4)
        qd_ref[:, sl] = (qd * (DIFF_SCALE * LOG2E)).astype(BF16)
        kd_ref[:, sl] = kd.astype(BF16)
        vdt_ref[0, sl, :] = vd.T.astype(BF16)
    zb_ref[:, ZB_DG:ZB_DG + DIFF_COLS] = proj(wb_ref, B_DG, B_COLS)


def _kv_specs(tk):
    rows = lambda width: pl.BlockSpec((tk, width), lambda i: (i, 0))
    tr = lambda height: pl.BlockSpec((1, height, tk), lambda i: (i, 0, 0))
    return [rows(MLA_H * HEAD_PAD), tr(MLA_H * MLA_V), rows(DIFF_COLS), tr(DIFF_COLS)]


def _kv_shapes(n_blocks, tk):
    return [jax.ShapeDtypeStruct((n_blocks * tk, MLA_H * HEAD_PAD), BF16),
            jax.ShapeDtypeStruct((n_blocks, MLA_H * MLA_V, tk), BF16),
            jax.ShapeDtypeStruct((n_blocks * tk, DIFF_COLS), BF16),
            jax.ShapeDtypeStruct((n_blocks, DIFF_COLS, tk), BF16)]


def _inproj(x, mod, tabs, lw, layer, seq, rows_per_mod, tk, caches, prev_caches=()):
    t, d = x.shape
    tm = tk
    tiles_per_mod = rows_per_mod // tm
    rope = tabs is not None
    in_specs = [pl.BlockSpec((tm, d), lambda i: (i, 0)),
                pl.BlockSpec((1, 3, d), lambda i: (i // tiles_per_mod, 0, 0)),
                _layer_spec((1, d), layer),
                _layer_spec((d, A_COLS), layer), _layer_spec((d, LANES), layer), _layer_spec((d, B_COLS), layer)]
    args = [x, mod, lw['g_pre'], lw['w_a'], lw['w_krp'], lw['w_b']]
    if rope:
        tab_spec = pl.BlockSpec((3, tm, LANES), lambda i: (0, i % (seq // tm), 0))
        in_specs += [tab_spec, tab_spec]
        args += [tabs[0], tabs[1]]
    in_specs += [_layer_spec((1, Q_RANK), layer), _layer_spec((Q_RANK, MLA_H * HEAD_PAD), layer),
                 _layer_spec((1, KV_RANK), layer), _layer_spec((KV_RANK, MLA_H * HEAD_PAD), layer),
                 _layer_spec((MLA_H * MLA_V, KV_RANK), layer)]
    args += [lw['q_norm'], lw['wuq'], lw['kv_norm'], lw['wuk'], lw['wuvt']]
    row = lambda width: pl.BlockSpec((tm, width), lambda i: (i, 0))
    heads = pl.BlockSpec((tm, DIFF_H, DIFF_DV), lambda i: (i, 0, 0))
    n_prev = len(prev_caches)
    for prev in prev_caches:
        in_specs += [row(MLA_ROPE), heads, heads, row(KV_RANK)]
        args += list(prev)
    f32 = lambda width: jax.ShapeDtypeStruct((t, width), F32)
    bf16 = lambda width: jax.ShapeDtypeStruct((t, width), BF16)
    k_spec, vt_spec, kd_spec, vdt_spec = _kv_specs(tk)
    k_shape, vt_shape, kd_shape, vdt_shape = _kv_shapes(t // tk, tk)
    qk = MLA_H * HEAD_PAD
    z_specs = [row(A_LAT), row(ZB_DG + DIFF_COLS)]
    z_shapes = [f32(A_LAT), f32(ZB_DG + DIFF_COLS)]
    ckvn_spec, ckvn_shape = row(KV_RANK), f32(KV_RANK)
    if caches and not n_prev:
        z_specs += [row(MLA_ROPE), heads, heads]
        z_shapes += [f32(MLA_ROPE)] + [jax.ShapeDtypeStruct((t, DIFF_H, DIFF_DV), F32)] * 2
    elif caches:
        per = seq // tm

        def stacked(*minor):
            spec = pl.BlockSpec((1, n_prev + 1, tm) + minor, lambda i: (i // per, 0, i % per) + (0,) * len(minor))
            return spec, jax.ShapeDtypeStruct((t // seq, n_prev + 1, seq) + minor, F32)

        (kr_spec, kr_shape), (hd_spec, hd_shape) = stacked(MLA_ROPE), stacked(DIFF_H, DIFF_DV)
        ckvn_spec, ckvn_shape = stacked(KV_RANK)
        z_specs += [kr_spec, hd_spec, hd_spec]
        z_shapes += [kr_shape, hd_shape, hd_shape]
    outs = pl.pallas_call(
        functools.partial(_inproj_kernel, rope=rope, caches=caches, n_prev=n_prev),
        grid=(t // tm,),
        in_specs=in_specs,
        out_specs=z_specs + [row(qk), k_spec, vt_spec, ckvn_spec, row(DIFF_COLS), kd_spec, vdt_spec],
        out_shape=z_shapes + [bf16(qk), k_shape, vt_shape, ckvn_shape, bf16(DIFF_COLS), kd_shape, vdt_shape],
        compiler_params=_params("arbitrary"),
        name="inproj",
    )(*args)
    zs, (q, k, vt, ckvn, qd, kd, vdt) = outs[:len(z_specs)], outs[len(z_specs):]
    return zs, q, ckvn, qd, (k, vt, kd, vdt)


def _ctx_kernel(ckvn_ref, kr_ref, dk_ref, dv_ref, wuk_ref, wuvt_ref, k_ref, vt_ref, kd_ref, vdt_ref):
    row = lax.broadcasted_iota(jnp.int32, (MLA_ROPE, LANES), 0)
    lane = lax.broadcasted_iota(jnp.int32, (MLA_ROPE, LANES), 1)
    place = (lane == row + MLA_NOPE).astype(BF16)
    krp = jnp.dot(kr_ref[...].astype(BF16), place, preferred_element_type=F32)
    _kv_up(ckvn_ref[...], krp, wuk_ref, wuvt_ref, k_ref, vt_ref)
    for h in range(DIFF_H):
        sl = slice(h * LANES, (h + 1) * LANES)
        kd_ref[:, sl] = dk_ref[:, h, :].astype(BF16)
        vdt_ref[0, sl, :] = dv_ref[:, h, :].T.astype(BF16)


def _ctx(ckvn, kr, dk, dv, lw, layer, tk):
    nb, _, past, _ = ckvn.shape
    n_past = past // tk
    cached = lambda *minor: pl.BlockSpec((None, None, tk) + minor,
                                         lambda i: (i // n_past, layer, i % n_past) + (0,) * len(minor))
    return pl.pallas_call(
        _ctx_kernel,
        grid=(nb * n_past,),
        in_specs=[cached(KV_RANK), cached(MLA_ROPE), cached(DIFF_H, DIFF_DV), cached(DIFF_H, DIFF_DV),
                  _layer_spec((KV_RANK, MLA_H * HEAD_PAD), layer), _layer_spec((MLA_H * MLA_V, KV_RANK), layer)],
        out_specs=_kv_specs(tk),
        out_shape=_kv_shapes(nb * n_past, tk),
        compiler_params=_params("arbitrary"),
        name="ctx_kv",
    )(ckvn, kr, dk, dv, lw['wuk'], lw['wuvt'])


def _tile_scan(a, b, reverse):
    r_in_tile = lax.broadcasted_iota(jnp.int32, a.shape, 1)
    for k in (1, 2, 4):
        shift = SUBLANES - k if reverse else k
        valid = (r_in_tile < SUBLANES - k) if reverse else (r_in_tile >= k)
        a_s = jnp.where(valid, pltpu.roll(a, shift, 1), 1.0)
        b_s = jnp.where(valid, pltpu.roll(b, shift, 1), 0.0)
        b = a * b_s + b
        a = a * a_s
    return a, b


def _rnn_kernel(rx_ref, rg_ref, h0_ref, cw_ref, cb_ref, wg_ref, bg_ref, lam_ref, y_ref, st_ref,
                xpad, hf, hb, *, seq, rows):
    n_chunks = seq // rows
    n_tiles = rows // SUBLANES
    pad = SUBLANES

    xpad[0:pad, :] = jnp.zeros((pad, D_RNN), F32)
    xpad[seq + pad:seq + 2 * pad, :] = jnp.zeros((pad, D_RNN), F32)

    def copy_chunk(c, carry):
        r0 = pl.multiple_of(c * rows, rows)
        xpad[pl.ds(r0 + pad, rows), :] = rx_ref[pl.ds(r0, rows), :]
        return carry

    lax.fori_loop(0, n_chunks, copy_chunk, 0)

    def chunk_maps(r0, d):
        n = rows + 2 * pad
        xe = xpad[pl.ds(r0, n), :]
        taps = (pltpu.roll(xe, 2, 0), pltpu.roll(xe, 1, 0), xe, pltpu.roll(xe, n - 1, 0))
        xc = cb_ref[...]
        for k in range(CONV_W):
            xc = xc + taps[k][pad:pad + rows] * cw_ref[k:k + 1, :]
        t = jnp.tanh(jnp.dot(xc.astype(BF16), wg_ref[d], preferred_element_type=F32) + bg_ref[d])
        i = 0.5 * t[:, D_RNN:] + 0.5
        neg_lam = -lam_ref[d]
        softplus = jnp.maximum(neg_lam, 0.0) + jnp.log1p(jnp.exp(-jnp.abs(neg_lam)))
        c = (-0.5 * RG_C) * softplus
        log_a = c * t[:, :D_RNN] + c
        a = jnp.exp(log_a)
        b = jnp.sqrt(-jnp.tanh(log_a) * (1.0 + a * a)) * (i * xc)
        return a.reshape(n_tiles, SUBLANES, D_RNN), b.reshape(n_tiles, SUBLANES, D_RNN)

    def fwd_chunk(c, carry):
        r0 = pl.multiple_of(c * rows, rows)
        a3, b3 = _tile_scan(*chunk_maps(r0, 0), reverse=False)
        for j in range(n_tiles):
            ht = a3[j] * carry + b3[j]
            hf[pl.ds(r0 + j * SUBLANES, SUBLANES), :] = ht
            carry = ht[SUBLANES - 1:SUBLANES, :]
        return carry

    def bwd_chunk(cc, carry):
        r0 = pl.multiple_of((n_chunks - 1 - cc) * rows, rows)
        a3, b3 = _tile_scan(*chunk_maps(r0, 1), reverse=True)
        for j in reversed(range(n_tiles)):
            ht = a3[j] * carry + b3[j]
            hb[j * SUBLANES:(j + 1) * SUBLANES, :] = ht
            carry = ht[0:1, :]
        y = (hf[pl.ds(r0, rows), :] + hb[...]) * _silu(rg_ref[pl.ds(r0, rows), :])
        y_ref[pl.ds(r0, rows), :] = y.astype(y_ref.dtype)
        return carry

    st_ref[0, 0:1, :] = lax.fori_loop(0, n_chunks, fwd_chunk, h0_ref[0, 0:1, :])
    st_ref[0, 1:2, :] = lax.fori_loop(0, n_chunks, bwd_chunk, h0_ref[0, 1:2, :])


def _rnn(za, h0, lw, layer, nb, seq):
    t = za.shape[0]
    rows = 256
    kern = functools.partial(_rnn_kernel, seq=seq, rows=rows)
    return pl.pallas_call(
        kern,
        grid=(nb,),
        in_specs=[pl.BlockSpec((seq, D_RNN), lambda b: (b, 0)),
                  pl.BlockSpec((seq, D_RNN), lambda b: (b, 1)),
                  pl.BlockSpec((1, 2, D_RNN), lambda b: (b, 0, 0)),
                  _layer_spec((CONV_W, D_RNN), layer),
                  _layer_spec((1, D_RNN), layer),
                  _layer_spec((2, D_RNN, 2 * D_RNN), layer),
                  _layer_spec((2, 1, 2 * D_RNN), layer),
                  _layer_spec((2, 1, D_RNN), layer)],
        out_specs=[pl.BlockSpec((seq, D_RNN), lambda b: (b, 0)),
                   pl.BlockSpec((1, 2, D_RNN), lambda b: (b, 0, 0))],
        out_shape=[jax.ShapeDtypeStruct((t, D_RNN), BF16),
                   jax.ShapeDtypeStruct((nb, 2, D_RNN), F32)],
        scratch_shapes=[pltpu.VMEM((seq + 2 * SUBLANES, D_RNN), F32),
                        pltpu.VMEM((seq, D_RNN), F32),
                        pltpu.VMEM((rows, D_RNN), F32)],
        compiler_params=_params("arbitrary"),
        name="rglru",
    )(za, za, h0, lw['conv_w'], lw['conv_b'], lw['wg'], lw['bg'], lw['lam'])


def _attend_t(chains, kv_refs, tk):
    blocks = [(k_ref, vt_ref, j) for k_ref, vt_ref in kv_refs for j in range(vt_ref.shape[0])]

    def scores(block):
        k_ref, _, j = block
        k_blocks = {}
        out = []
        for q, kcols, _ in chains:
            key = (kcols.start, kcols.stop)
            if key not in k_blocks:
                k_blocks[key] = k_ref[j * tk:(j + 1) * tk, kcols]
            out.append(_nt_dot(k_blocks[key], q))
        return out

    state = [None] * len(chains)
    s_cur = scores(blocks[0])
    for n, (_, vt_ref, j) in enumerate(blocks):
        s_next = scores(blocks[n + 1]) if n + 1 < len(blocks) else None
        for c, (_, _, vrows) in enumerate(chains):
            st = s_cur[c]
            m_blk = jnp.max(st, axis=0, keepdims=True)
            if state[c] is None:
                m_new = m_blk
            else:
                m, l, acc = state[c]
                m_new = jnp.maximum(m, m_blk)
                alpha = jnp.exp2(m - m_new)
            p = jnp.exp2(st - m_new)
            l_blk = jnp.sum(p, axis=0, keepdims=True)
            pv = jnp.dot(vt_ref[j, vrows, :], p.astype(BF16), preferred_element_type=F32)
            state[c] = (m_new, l_blk, pv) if state[c] is None else (m_new, alpha * l + l_blk, alpha * acc + pv)
        s_cur = s_next
    return [acc / l for _, l, acc in state]


def _kv_in_specs(kv_parts, nb, tk, k_width, vt_rows):
    specs, args = [], []
    for k, vt in kv_parts:
        n = vt.shape[0] // nb
        specs += [pl.BlockSpec((n * tk, k_width), lambda b, h, i: (b, h)),
                  pl.BlockSpec((n, vt_rows, tk), lambda b, h, i: (b, h, 0))]
        args += [k, vt]
    return specs, args


def _gate_specs(col0, width, tq, row_block):
    w = math.gcd(col0, width) if col0 else width
    return [pl.BlockSpec((tq, w), lambda b, h, i, n=n: (row_block(b, i), (col0 + h * width) // w + n))
            for n in range(width // w)]


def _gate(refs, rows):
    return jnp.concatenate([r[rows, :] for r in refs], axis=1)


def _mla_attn_kernel(q_ref, *refs, tk, n_parts, n_sub, n_heads):
    kv_refs = [(refs[2 * n], refs[2 * n + 1]) for n in range(n_parts)]
    mg_refs, o_ref = refs[2 * n_parts:-1], refs[-1]
    chains = [(q_ref[:, h * HEAD_PAD:(h + 1) * HEAD_PAD], slice(h * HEAD_PAD, (h + 1) * HEAD_PAD),
               slice(h * MLA_V, (h + 1) * MLA_V)) for h in range(n_heads)]
    ot = jnp.concatenate(_attend_t(chains, kv_refs, tk), axis=0)
    rows = slice(0, n_sub * TQ)
    o_ref[...] = (ot.T * _silu(_gate(mg_refs, rows))).astype(o_ref.dtype)


def _attn_tiling(seq, n_kv, heads):
    if n_kv == 1:
        return 1, heads
    return min(4, seq // TQ), None


def _mla_attn(q, kv_parts, zb, nb, seq, tk):
    t = q.shape[0]
    n_kv = sum(vt.shape[0] for _, vt in kv_parts) // nb
    n_sub, n_heads = _attn_tiling(seq, n_kv, MLA_H)
    n_heads = n_heads or 2
    tq = n_sub * TQ
    nq = seq // tq
    kern = functools.partial(_mla_attn_kernel, tk=tk, n_parts=len(kv_parts), n_sub=n_sub, n_heads=n_heads)
    kv_specs, kv_args = _kv_in_specs(kv_parts, nb, tk, n_heads * HEAD_PAD, n_heads * MLA_V)
    mg_specs = _gate_specs(ZB_MG, n_heads * MLA_V, tq, lambda b, i: b * nq + i)
    return pl.pallas_call(
        kern,
        grid=(nb, MLA_H // n_heads, nq),
        in_specs=[pl.BlockSpec((tq, n_heads * HEAD_PAD), lambda b, h, i: (b * nq + i, h))] + kv_specs + mg_specs,
        out_specs=pl.BlockSpec((tq, n_heads * MLA_V), lambda b, h, i: (b * nq + i, h)),
        out_shape=jax.ShapeDtypeStruct((t, MLA_H * MLA_V), BF16),
        compiler_params=_params("arbitrary", "arbitrary", "arbitrary"),
        name="mla_attn",
    )(q, *kv_args, *([zb] * len(mg_specs)))


def _diff_attn_kernel(q_ref, lam_ref, dn_ref, *refs, tk, n_parts, n_sub, n_heads, lam_init):
    kv_refs = [(refs[2 * n], refs[2 * n + 1]) for n in range(n_parts)]
    dg_refs, o_ref = refs[2 * n_parts:-1], refs[-1]
    lp = lam_ref[...]
    lam = (jnp.exp(jnp.sum(lp[0:1] * lp[1:2], axis=-1, keepdims=True))
           - jnp.exp(jnp.sum(lp[2:3] * lp[3:4], axis=-1, keepdims=True)) + lam_init)
    chains = []
    for s in range(n_sub):
        for h in range(n_heads):
            cols = slice(h * LANES, (h + 1) * LANES)
            q = q_ref[s * TQ:(s + 1) * TQ, cols].astype(F32)
            lane = lax.broadcasted_iota(jnp.int32, q.shape, 1)
            q12 = jnp.concatenate([jnp.where(lane < DIFF_DK, q, 0.0), jnp.where(lane < DIFF_DK, 0.0, q)], axis=0)
            chains.append((q12.astype(BF16), cols, cols))
    outs = _attend_t(chains, kv_refs, tk)
    for s in range(n_sub):
        rows = slice(s * TQ, (s + 1) * TQ)
        heads = []
        for h in range(n_heads):
            o12 = outs[s * n_heads + h]
            od = (o12[:, :TQ] - lam * o12[:, TQ:]).T
            heads.append(_rms(od) * dn_ref[...] * (1.0 - lam_init))
        od = heads[0] if n_heads == 1 else jnp.concatenate(heads, axis=1)
        o_ref[rows, :] = (od * _silu(_gate(dg_refs, rows))).astype(o_ref.dtype)


def _diff_attn(qd, kv_parts, zb, lw, layer, nb, seq, tk):
    t = qd.shape[0]
    n_kv = sum(vt.shape[0] for _, vt in kv_parts) // nb
    n_sub, n_heads = _attn_tiling(seq, n_kv, DIFF_H)
    n_heads = n_heads or 1
    tq = n_sub * TQ
    nq = seq // tq
    lam_init = 0.8 - 0.6 * math.exp(-0.3 * layer)
    kern = functools.partial(_diff_attn_kernel, tk=tk, n_parts=len(kv_parts), n_sub=n_sub, n_heads=n_heads,
                             lam_init=lam_init)
    width = n_heads * LANES
    kv_specs, kv_args = _kv_in_specs(kv_parts, nb, tk, width, width)
    dg_specs = _gate_specs(ZB_DG, width, tq, lambda b, i: b * nq + i)
    return pl.pallas_call(
        kern,
        grid=(nb, DIFF_H // n_heads, nq),
        in_specs=[pl.BlockSpec((tq, width), lambda b, h, i: (b * nq + i, h)),
                  _layer_spec((4, DIFF_DK), layer),
                  _layer_spec((1, DIFF_DV), layer)] + kv_specs + dg_specs,
        out_specs=pl.BlockSpec((tq, width), lambda b, h, i: (b * nq + i, h)),
        out_shape=jax.ShapeDtypeStruct((t, DIFF_COLS), BF16),
        compiler_params=_params("arbitrary", "arbitrary", "arbitrary"),
        name="diff_attn",
    )(qd, lw['lam_p'], lw['diff_norm'], *kv_args, *([zb] * len(dg_specs)))


def _merge_kernel(yr_ref, ym_ref, yd_ref, x_ref, mod_ref, gpre_ref, wg_ref, wr_ref, wm_ref, wd_ref, wo_ref,
                  gpost_ref, o_ref):
    x = x_ref[...]
    h = _modulated_norm(x, gpre_ref, mod_ref).astype(BF16)
    merged = None
    for k, (y_ref, w_ref) in enumerate(((yr_ref, wr_ref), (ym_ref, wm_ref), (yd_ref, wd_ref))):
        gate = _sigmoid(jnp.dot(h, wg_ref[:, k * D_MODEL:(k + 1) * D_MODEL], preferred_element_type=F32))
        term = gate * jnp.dot(y_ref[...], w_ref[...], preferred_element_type=F32)
        merged = term if merged is None else merged + term
    o = jnp.dot(merged.astype(BF16), wo_ref[...], preferred_element_type=F32)
    o_ref[...] = x + mod_ref[0, 2:3, :] * (_rms(o) * gpost_ref[...])


def _merge(y_rnn, y_mla, y_diff, x, mod, lw, layer, rows_per_mod):
    t, d = x.shape
    tm = 512
    tiles_per_mod = rows_per_mod // tm
    row = lambda width: pl.BlockSpec((tm, width), lambda i: (i, 0))
    return pl.pallas_call(
        _merge_kernel,
        grid=(t // tm,),
        in_specs=[row(D_RNN), row(MLA_H * MLA_V), row(DIFF_COLS), row(d),
                  pl.BlockSpec((1, 3, d), lambda i: (i // tiles_per_mod, 0, 0)),
                  _layer_spec((1, d), layer), _layer_spec((d, 3 * d), layer),
                  _layer_spec((D_RNN, d), layer), _layer_spec((MLA_H * MLA_V, d), layer),
                  _layer_spec((DIFF_COLS, d), layer), _layer_spec((d, d), layer), _layer_spec((1, d), layer)],
        out_specs=row(d),
        out_shape=jax.ShapeDtypeStruct((t, d), F32),
        compiler_params=_params("arbitrary"),
        name="merge_out",
    )(y_rnn, y_mla, y_diff, x, mod, lw['g_pre'], lw['w_mgate'], lw['w_br_rnn'], lw['w_br_mla'], lw['w_br_diff'],
      lw['w_out'], lw['g_post'])


def _pad_heads(w, width):
    l, k, _ = w.shape
    w4 = jnp.pad(w.reshape(l, k, MLA_H, width), ((0, 0), (0, 0), (0, 0), (0, HEAD_PAD - width)))
    return w4.reshape(l, k, MLA_H * HEAD_PAD)


def _layer_params(p):
    depth, d, _ = p['w_in'].shape
    w_in = p['w_in']
    kr_pad = ((0, 0), (0, 0), (MLA_NOPE, HEAD_PAD - MLA_NOPE - MLA_ROPE))
    eye = jnp.eye(RNN_BLOCKS, dtype=F32)[None, None, :, None, :, None]
    dense = lambda w: (w[:, :, :, :, None, :] * eye).reshape(depth, 2, D_RNN, D_RNN)
    ukv = p['w_ukv'].reshape(depth, KV_RANK, MLA_H, MLA_NOPE + MLA_V)
    return dict(
        g_pre=p['g_pre'].reshape(depth, 1, d), g_post=p['g_post'].reshape(depth, 1, d),
        w_a=w_in[:, :, :_O_KR].astype(BF16), w_krp=jnp.pad(w_in[:, :, _O_KR:_O_MG], kr_pad).astype(BF16),
        w_b=w_in[:, :, _O_MG:_O_MGATE].astype(BF16), w_mgate=w_in[:, :, _O_MGATE:].astype(BF16),
        conv_w=p['conv_w'], conv_b=p['conv_b'].reshape(depth, 1, D_RNN),
        wg=(0.5 * jnp.concatenate([dense(p['w_rg_a']), dense(p['w_rg_x'])], axis=-1)).astype(BF16),
        bg=0.5 * jnp.concatenate([p['b_rg_a'], p['b_rg_x']], axis=-1).reshape(depth, 2, 1, 2 * D_RNN),
        lam=p['rg_lam'].reshape(depth, 2, 1, D_RNN),
        q_norm=p['q_norm'].reshape(depth, 1, Q_RANK), kv_norm=p['kv_norm'].reshape(depth, 1, KV_RANK),
        wuq=_pad_heads(p['w_uq'], MLA_NOPE + MLA_ROPE).astype(BF16),
        wuk=_pad_heads(ukv[..., :MLA_NOPE].reshape(depth, KV_RANK, MLA_H * MLA_NOPE), MLA_NOPE).astype(BF16),
        wuvt=jnp.swapaxes(ukv[..., MLA_NOPE:].reshape(depth, KV_RANK, MLA_H * MLA_V), 1, 2).astype(BF16),
        lam_p=jnp.stack([p['lam_q1'], p['lam_k1'], p['lam_q2'], p['lam_k2']], axis=1),
        diff_norm=p['diff_norm'].reshape(depth, 1, DIFF_DV),
        w_br_rnn=p['w_br_rnn'].astype(BF16), w_br_mla=p['w_br_mla'].astype(BF16),
        w_br_diff=p['w_br_diff'].astype(BF16), w_out=p['w_out'].astype(BF16))


def _rope_tables(n_pos, groups, dim):
    half = dim // 4
    inv = ROPE_THETA ** (-np.arange(half, dtype=np.float32) / half)
    inv_row = np.zeros((LANES,), np.float32)
    inv_col = np.zeros((LANES,), np.float32)
    first = np.zeros((LANES,), np.float32)
    second = np.zeros((LANES,), np.float32)
    for off in groups:
        for inv_axis, base in ((inv_row, off), (inv_col, off + 2 * half)):
            inv_axis[base:base + half] = inv
            inv_axis[base + half:base + 2 * half] = inv
            first[base:base + half] = 1.0
            second[base + half:base + 2 * half] = 1.0
    pos = jnp.arange(n_pos, dtype=jnp.int32)
    ang = ((pos // GRID_W).astype(F32)[:, None] * jnp.asarray(inv_row)[None, :]
           + (pos % GRID_W).astype(F32)[:, None] * jnp.asarray(inv_col)[None, :])
    sin = jnp.sin(ang)
    return jnp.stack([jnp.cos(ang), -sin * jnp.asarray(first), sin * jnp.asarray(second)])


def _layer(l, x, mod, lw, nb, seq, rows_per_mod, h0, tabs, ctx, prev_caches=()):
    tk = min(TK, seq)
    zs, q, ckvn, qd, (k, vt, kd, vdt) = _inproj(x, mod, tabs, lw, l, seq, rows_per_mod, tk, ctx is None,
                                                 prev_caches)
    y_rnn, st = _rnn(zs[0], h0, lw, l, nb, seq)
    mla_kv, diff_kv = [(k, vt)], [(kd, vdt)]
    if ctx is not None:
        kc, vtc, kdc, vdtc = _ctx(*ctx, lw, l, tk)
        mla_kv, diff_kv = [(kc, vtc)] + mla_kv, [(kdc, vdtc)] + diff_kv
    y_mla = _mla_attn(q, mla_kv, zs[1], nb, seq, tk)
    y_diff = _diff_attn(qd, diff_kv, zs[1], lw, l, nb, seq, tk)
    x_new = _merge(y_rnn, y_mla, y_diff, x, mod, lw, l, rows_per_mod)
    return x_new, zs[2:], ckvn, st


def kernel(x_prompt, x_sample, cache_mla_ckv, cache_mla_krope, cache_diff_k, cache_diff_v, state_rnn, c, c_ctx, w_mod, b_mod, g_pre, g_post, w_in, conv_w, conv_b, w_rg_a, b_rg_a, w_rg_x, b_rg_x, rg_lam, q_norm, w_uq, kv_norm, w_ukv, lam_q1, lam_k1, lam_q2, lam_k2, diff_norm, w_br_rnn, w_br_mla, w_br_diff, w_out):
    depth = w_in.shape[0]
    nbp, sp, d = x_prompt.shape
    nbs, ss, _ = x_sample.shape

    n_cond = -(-(1 + nbs) // SUBLANES) * SUBLANES
    cond = jnp.concatenate([c_ctx[None, :], c, jnp.zeros((n_cond - 1 - nbs, d), F32)], axis=0)
    mod = _modulation(cond, w_mod, b_mod).reshape(depth, n_cond, 3, d)

    lw = _layer_params(dict(
        w_in=w_in, g_pre=g_pre, g_post=g_post, conv_w=conv_w, conv_b=conv_b, w_rg_a=w_rg_a, b_rg_a=b_rg_a,
        w_rg_x=w_rg_x, b_rg_x=b_rg_x, rg_lam=rg_lam, q_norm=q_norm, w_uq=w_uq, kv_norm=kv_norm, w_ukv=w_ukv,
        lam_q1=lam_q1, lam_k1=lam_k1, lam_q2=lam_q2, lam_k2=lam_k2, diff_norm=diff_norm, w_br_rnn=w_br_rnn,
        w_br_mla=w_br_mla, w_br_diff=w_br_diff, w_out=w_out))

    xp = x_prompt.reshape(nbp * sp, d)
    h0 = jnp.zeros((nbp, 2, D_RNN), F32)
    caches, st_l = [], []
    for l in range(depth):
        xp, (krc, dkc, dvc), ckvn, st = _layer(l, xp, mod[l, 0:1], lw, nbp, sp, nbp * sp, h0, None, None,
                                               caches if l == depth - 1 else ())
        caches.append((krc, dkc, dvc, ckvn))
        st_l.append(st)
    new_kr, new_dk, new_dv, new_ckv = caches[-1]

    tabs = (_rope_tables(ss, (MLA_NOPE,), MLA_ROPE), _rope_tables(ss, (0, DIFF_DK), DIFF_DK))
    ctx = (cache_mla_ckv, cache_mla_krope, cache_diff_k, cache_diff_v)
    xs = x_sample.reshape(nbs * ss, d)
    for l in range(depth):
        xs, _, _, _ = _layer(l, xs, mod[l, 1:1 + nbs], lw, nbs, ss, ss, state_rnn[:, l], tabs, ctx)

    return (xp.reshape(nbp, sp, d), xs.reshape(nbs, ss, d), new_ckv, new_kr, new_dk, new_dv, jnp.stack(st_l, axis=1))
```

```python
import functools
import math

import jax
import jax.numpy as jnp
import numpy as np
from jax import lax
from jax.experimental import pallas as pl
from jax.experimental.pallas import tpu as pltpu

F32 = jnp.float32
BF16 = jnp.bfloat16

D_MODEL = 1024
GRID_W = 64
EPS = 1e-6
ROPE_THETA = 10000.0
LOG2E = math.log2(math.e)

D_RNN = 512
RNN_BLOCKS = 8
CONV_W = 4
RG_C = 8.0

MLA_H = 8
MLA_NOPE = 64
MLA_ROPE = 32
MLA_V = 64
Q_RANK = 384
KV_RANK = 256
MLA_SCALE = (MLA_NOPE + MLA_ROPE) ** -0.5

DIFF_H = 4
DIFF_DK = 64
DIFF_DV = 2 * DIFF_DK
DIFF_SCALE = DIFF_DK ** -0.5
DIFF_COLS = DIFF_H * DIFF_DV

LANES = 128
SUBLANES = 8
HEAD_PAD = LANES
TQ = 256
TK = 512
VMEM_LIMIT = 56 * 1024 * 1024

_O_KR, _O_MG, _O_MGATE = 1664, 1696, 4256
A_RNN, A_LAT, A_COLS = 0, 1024, 1664
B_MG, B_QKV, B_DG, B_COLS = 0, 512, 2048, 2560
ZB_MG = 0
ZB_DG = 512


def _sigmoid(x):
    return 0.5 * jnp.tanh(0.5 * x) + 0.5


def _silu(x):
    return x * _sigmoid(x)


def _rms(x):
    return x * lax.rsqrt(jnp.mean(x * x, axis=-1, keepdims=True) + EPS)


def _params(*sem):
    return pltpu.CompilerParams(dimension_semantics=sem, vmem_limit_bytes=VMEM_LIMIT)


def _layer_spec(shape, layer):
    return pl.BlockSpec((None,) + tuple(shape), lambda *_: (layer,) + (0,) * len(shape))


def _mod_kernel(cond_ref, w_ref, b_ref, o_ref):
    c = cond_ref[...]
    o_ref[0] = jnp.dot(_silu(c).astype(BF16), w_ref[0].astype(BF16), preferred_element_type=F32) + b_ref[0]


def _modulation(cond, w_mod, b_mod):
    depth, d, n = w_mod.shape
    r = cond.shape[0]
    tn = 1024
    return pl.pallas_call(
        _mod_kernel,
        grid=(depth, n // tn),
        in_specs=[pl.BlockSpec((r, d), lambda l, j: (0, 0)),
                  pl.BlockSpec((1, d, tn), lambda l, j: (l, 0, j)),
                  pl.BlockSpec((1, 1, tn), lambda l, j: (l, 0, j))],
        out_specs=pl.BlockSpec((1, r, tn), lambda l, j: (l, 0, j)),
        out_shape=jax.ShapeDtypeStruct((depth, r, n), F32),
        compiler_params=_params("arbitrary", "arbitrary"),
        name="modulation",
    )(cond, w_mod, b_mod.reshape(depth, 1, n))


def _modulated_norm(x, g_ref, mod_ref):
    return (_rms(x) * g_ref[...]) * (1.0 + mod_ref[0, 1:2, :]) + mod_ref[0, 0:1, :]


def _rope(x, tab_ref, half):
    n = x.shape[-1]
    return x * tab_ref[0] + pltpu.roll(x, n - half, 1) * tab_ref[1] + pltpu.roll(x, half, 1) * tab_ref[2]


def _nt_dot(a, b):
    return lax.dot_general(a, b, (((1,), (1,)), ((), ())), preferred_element_type=F32)


def _kv_up(ckvn, krp, wuk_ref, wuvt_ref, k_ref, vt_ref):
    cb = ckvn.astype(BF16)
    kn = jnp.dot(cb, wuk_ref[...], preferred_element_type=F32)
    vt_ref[0] = _nt_dot(wuvt_ref[...], cb).astype(BF16)
    for h in range(MLA_H):
        sl = slice(h * HEAD_PAD, (h + 1) * HEAD_PAD)
        k_ref[:, sl] = (kn[:, sl] + krp).astype(BF16)


def _inproj_kernel(*refs, rope, caches, n_prev):
    refs = list(refs)
    x_ref, mod_ref, g_ref, wa_ref, wkr_ref, wb_ref = refs[:6]
    tabm_ref, tabd_ref = (refs[6], refs[7]) if rope else (None, None)
    n_in = (13 if rope else 11) + 4 * n_prev
    qn_ref, wuq_ref, kvn_ref, wuk_ref, wuvt_ref = refs[n_in - 4 * n_prev - 5:n_in - 4 * n_prev]
    prev = refs[n_in - 4 * n_prev:n_in]
    outs = refs[n_in:]
    za_ref, zb_ref = outs[:2]
    krc_ref, dkc_ref, dvc_ref = outs[2:5] if caches else (None, None, None)
    q_ref, k_ref, vt_ref, ckvn_ref, qd_ref, kd_ref, vdt_ref = outs[5:] if caches else outs[2:]
    if n_prev:
        for l in range(n_prev):
            for dst, src in zip((krc_ref, dkc_ref, dvc_ref, ckvn_ref), prev[4 * l:4 * l + 4]):
                dst[0, l] = src[...]
        krc_ref, dkc_ref, dvc_ref, ckvn_ref = (r.at[0, n_prev] for r in (krc_ref, dkc_ref, dvc_ref, ckvn_ref))

    h = _modulated_norm(x_ref[...], g_ref, mod_ref).astype(BF16)
    proj = lambda w_ref, c0, c1: jnp.dot(h, w_ref[:, c0:c1], preferred_element_type=F32)
    za_ref[...] = proj(wa_ref, A_RNN, A_LAT)

    lat = proj(wa_ref, A_LAT, A_COLS)
    cqn = _rms(lat[:, 0:Q_RANK]) * qn_ref[...]
    q = jnp.dot(cqn.astype(BF16), wuq_ref[...], preferred_element_type=F32) * (MLA_SCALE * LOG2E)
    ckvn = _rms(lat[:, Q_RANK:Q_RANK + KV_RANK]) * kvn_ref[...]
    ckvn_ref[...] = ckvn
    krp = jnp.dot(h, wkr_ref[...], preferred_element_type=F32)
    if caches:
        krc_ref[...] = krp[:, MLA_NOPE:MLA_NOPE + MLA_ROPE]
    if rope:
        krp = _rope(krp, tabm_ref, MLA_ROPE // 4)
    _kv_up(ckvn, krp, wuk_ref, wuvt_ref, k_ref, vt_ref)
    for hd in range(MLA_H):
        sl = slice(hd * HEAD_PAD, (hd + 1) * HEAD_PAD)
        qh = q[:, sl]
        if rope:
            qh = _rope(qh, tabm_ref, MLA_ROPE // 4)
        q_ref[:, sl] = qh.astype(BF16)
    zb_ref[:, ZB_MG:ZB_MG + MLA_H * MLA_V] = proj(wb_ref, B_MG, B_QKV)

    qkv = proj(wb_ref, B_QKV, B_DG)
    for hd in range(DIFF_H):
        sl = slice(hd * LANES, (hd + 1) * LANES)
        qd = qkv[:, sl]
        kd = qkv[:, DIFF_COLS + hd * LANES:DIFF_COLS + (hd + 1) * LANES]
        vd = qkv[:, 2 * DIFF_COLS + hd * LANES:2 * DIFF_COLS + (hd + 1) * LANES]
        if caches:
            dkc_ref[:, hd, :] = kd
            dvc_ref[:, hd, :] = vd
        if rope:
            qd = _rope(qd, tabd_ref, DIFF_DK // 4)
            kd = _rope(kd, tabd_ref, DIFF_DK // 4)
        qd_ref[:, sl] = (qd * (DIFF_SCALE * LOG2E)).astype(BF16)
        kd_ref[:, sl] = kd.astype(BF16)
        vdt_ref[0, sl, :] = vd.T.astype(BF16)
    zb_ref[:, ZB_DG:ZB_DG + DIFF_COLS] = proj(wb_ref, B_DG, B_COLS)


def _kv_specs(tk):
    rows = lambda width: pl.BlockSpec((tk, width), lambda i: (i, 0))
    tr = lambda height: pl.BlockSpec((1, height, tk), lambda i: (i, 0, 0))
    return [rows(MLA_H * HEAD_PAD), tr(MLA_H * MLA_V), rows(DIFF_COLS), tr(DIFF_COLS)]


def _kv_shapes(n_blocks, tk):
    return [jax.ShapeDtypeStruct((n_blocks * tk, MLA_H * HEAD_PAD), BF16),
            jax.ShapeDtypeStruct((n_blocks, MLA_H * MLA_V, tk), BF16),
            jax.ShapeDtypeStruct((n_blocks * tk, DIFF_COLS), BF16),
            jax.ShapeDtypeStruct((n_blocks, DIFF_COLS, tk), BF16)]


def _inproj(x, mod, tabs, lw, layer, seq, rows_per_mod, tk, caches, prev_caches=()):
    t, d = x.shape
    tm = tk
    tiles_per_mod = rows_per_mod // tm
    rope = tabs is not None
    in_specs = [pl.BlockSpec((tm, d), lambda i: (i, 0)),
                pl.BlockSpec((1, 3, d), lambda i: (i // tiles_per_mod, 0, 0)),
                _layer_spec((1, d), layer),
                _layer_spec((d, A_COLS), layer), _layer_spec((d, LANES), layer), _layer_spec((d, B_COLS), layer)]
    args = [x, mod, lw['g_pre'], lw['w_a'], lw['w_krp'], lw['w_b']]
    if rope:
        tab_spec = pl.BlockSpec((3, tm, LANES), lambda i: (0, i % (seq // tm), 0))
        in_specs += [tab_spec, tab_spec]
        args += [tabs[0], tabs[1]]
    in_specs += [_layer_spec((1, Q_RANK), layer), _layer_spec((Q_RANK, MLA_H * HEAD_PAD), layer),
                 _layer_spec((1, KV_RANK), layer), _layer_spec((KV_RANK, MLA_H * HEAD_PAD), layer),
                 _layer_spec((MLA_H * MLA_V, KV_RANK), layer)]
    args += [lw['q_norm'], lw['wuq'], lw['kv_norm'], lw['wuk'], lw['wuvt']]
    row = lambda width: pl.BlockSpec((tm, width), lambda i: (i, 0))
    heads = pl.BlockSpec((tm, DIFF_H, DIFF_DV), lambda i: (i, 0, 0))
    n_prev = len(prev_caches)
    for prev in prev_caches:
        in_specs += [row(MLA_ROPE), heads, heads, row(KV_RANK)]
        args += list(prev)
    f32 = lambda width: jax.ShapeDtypeStruct((t, width), F32)
    bf16 = lambda width: jax.ShapeDtypeStruct((t, width), BF16)
    k_spec, vt_spec, kd_spec, vdt_spec = _kv_specs(tk)
    k_shape, vt_shape, kd_shape, vdt_shape = _kv_shapes(t // tk, tk)
    qk = MLA_H * HEAD_PAD
    z_specs = [row(A_LAT), row(ZB_DG + DIFF_COLS)]
    z_shapes = [f32(A_LAT), f32(ZB_DG + DIFF_COLS)]
    ckvn_spec, ckvn_shape = row(KV_RANK), f32(KV_RANK)
    if caches and not n_prev:
        z_specs += [row(MLA_ROPE), heads, heads]
        z_shapes += [f32(MLA_ROPE)] + [jax.ShapeDtypeStruct((t, DIFF_H, DIFF_DV), F32)] * 2
    elif caches:
        per = seq // tm

        def stacked(*minor):
            spec = pl.BlockSpec((1, n_prev + 1, tm) + minor, lambda i: (i // per, 0, i % per) + (0,) * len(minor))
            return spec, jax.ShapeDtypeStruct((t // seq, n_prev + 1, seq) + minor, F32)

        (kr_spec, kr_shape), (hd_spec, hd_shape) = stacked(MLA_ROPE), stacked(DIFF_H, DIFF_DV)
        ckvn_spec, ckvn_shape = stacked(KV_RANK)
        z_specs += [kr_spec, hd_spec, hd_spec]
        z_shapes += [kr_shape, hd_shape, hd_shape]
    outs = pl.pallas_call(
        functools.partial(_inproj_kernel, rope=rope, caches=caches, n_prev=n_prev),
        grid=(t // tm,),
        in_specs=in_specs,
        out_specs=z_specs + [row(qk), k_spec, vt_spec, ckvn_spec, row(DIFF_COLS), kd_spec, vdt_spec],
        out_shape=z_shapes + [bf16(qk), k_shape, vt_shape, ckvn_shape, bf16(DIFF_COLS), kd_shape, vdt_shape],
        compiler_params=_params("arbitrary"),
        name="inproj",
    )(*args)
    zs, (q, k, vt, ckvn, qd, kd, vdt) = outs[:len(z_specs)], outs[len(z_specs):]
    return zs, q, ckvn, qd, (k, vt, kd, vdt)


def _ctx_kernel(ckvn_ref, kr_ref, dk_ref, dv_ref, wuk_ref, wuvt_ref, k_ref, vt_ref, kd_ref, vdt_ref):
    row = lax.broadcasted_iota(jnp.int32, (MLA_ROPE, LANES), 0)
    lane = lax.broadcasted_iota(jnp.int32, (MLA_ROPE, LANES), 1)
    place = (lane == row + MLA_NOPE).astype(BF16)
    krp = jnp.dot(kr_ref[...].astype(BF16), place, preferred_element_type=F32)
    _kv_up(ckvn_ref[...], krp, wuk_ref, wuvt_ref, k_ref, vt_ref)
    for h in range(DIFF_H):
        sl = slice(h * LANES, (h + 1) * LANES)
        kd_ref[:, sl] = dk_ref[:, h, :].astype(BF16)
        vdt_ref[0, sl, :] = dv_ref[:, h, :].T.astype(BF16)


def _ctx(ckvn, kr, dk, dv, lw, layer, tk):
    nb, _, past, _ = ckvn.shape
    n_past = past // tk
    cached = lambda *minor: pl.BlockSpec((None, None, tk) + minor,
                                         lambda i: (i // n_past, layer, i % n_past) + (0,) * len(minor))
    return pl.pallas_call(
        _ctx_kernel,
        grid=(nb * n_past,),
        in_specs=[cached(KV_RANK), cached(MLA_ROPE), cached(DIFF_H, DIFF_DV), cached(DIFF_H, DIFF_DV),
                  _layer_spec((KV_RANK, MLA_H * HEAD_PAD), layer), _layer_spec((MLA_H * MLA_V, KV_RANK), layer)],
        out_specs=_kv_specs(tk),
        out_shape=_kv_shapes(nb * n_past, tk),
        compiler_params=_params("arbitrary"),
        name="ctx_kv",
    )(ckvn, kr, dk, dv, lw['wuk'], lw['wuvt'])


def _tile_scan(a, b, reverse):
    r_in_tile = lax.broadcasted_iota(jnp.int32, a.shape, 1)
    for k in (1, 2, 4):
        shift = SUBLANES - k if reverse else k
        valid = (r_in_tile < SUBLANES - k) if reverse else (r_in_tile >= k)
        a_s = jnp.where(valid, pltpu.roll(a, shift, 1), 1.0)
        b_s = jnp.where(valid, pltpu.roll(b, shift, 1), 0.0)
        b = a * b_s + b
        a = a * a_s
    return a, b


def _rnn_kernel(rx_ref, rg_ref, h0_ref, cw_ref, cb_ref, wg_ref, bg_ref, lam_ref, y_ref, st_ref,
                xpad, hf, hb, *, seq, rows):
    n_chunks = seq // rows
    n_tiles = rows // SUBLANES
    pad = SUBLANES

    xpad[0:pad, :] = jnp.zeros((pad, D_RNN), F32)
    xpad[seq + pad:seq + 2 * pad, :] = jnp.zeros((pad, D_RNN), F32)

    def copy_chunk(c, carry):
        r0 = pl.multiple_of(c * rows, rows)
        xpad[pl.ds(r0 + pad, rows), :] = rx_ref[pl.ds(r0, rows), :]
        return carry

    lax.fori_loop(0, n_chunks, copy_chunk, 0)

    def chunk_maps(r0, d):
        n = rows + 2 * pad
        xe = xpad[pl.ds(r0, n), :]
        taps = (pltpu.roll(xe, 2, 0), pltpu.roll(xe, 1, 0), xe, pltpu.roll(xe, n - 1, 0))
        xc = cb_ref[...]
        for k in range(CONV_W):
            xc = xc + taps[k][pad:pad + rows] * cw_ref[k:k + 1, :]
        t = jnp.tanh(jnp.dot(xc.astype(BF16), wg_ref[d], preferred_element_type=F32) + bg_ref[d])
        i = 0.5 * t[:, D_RNN:] + 0.5
        neg_lam = -lam_ref[d]
        softplus = jnp.maximum(neg_lam, 0.0) + jnp.log1p(jnp.exp(-jnp.abs(neg_lam)))
        c = (-0.5 * RG_C) * softplus
        log_a = c * t[:, :D_RNN] + c
        a = jnp.exp(log_a)
        b = jnp.sqrt(-jnp.tanh(log_a) * (1.0 + a * a)) * (i * xc)
        return a.reshape(n_tiles, SUBLANES, D_RNN), b.reshape(n_tiles, SUBLANES, D_RNN)

    def fwd_chunk(c, carry):
        r0 = pl.multiple_of(c * rows, rows)
        a3, b3 = _tile_scan(*chunk_maps(r0, 0), reverse=False)
        for j in range(n_tiles):
            ht = a3[j] * carry + b3[j]
            hf[pl.ds(r0 + j * SUBLANES, SUBLANES), :] = ht
            carry = ht[SUBLANES - 1:SUBLANES, :]
        return carry

    def bwd_chunk(cc, carry):
        r0 = pl.multiple_of((n_chunks - 1 - cc) * rows, rows)
        a3, b3 = _tile_scan(*chunk_maps(r0, 1), reverse=True)
        for j in reversed(range(n_tiles)):
            ht = a3[j] * carry + b3[j]
            hb[j * SUBLANES:(j + 1) * SUBLANES, :] = ht
            carry = ht[0:1, :]
        y = (hf[pl.ds(r0, rows), :] + hb[...]) * _silu(rg_ref[pl.ds(r0, rows), :])
        y_ref[pl.ds(r0, rows), :] = y.astype(y_ref.dtype)
        return carry

    st_ref[0, 0:1, :] = lax.fori_loop(0, n_chunks, fwd_chunk, h0_ref[0, 0:1, :])
    st_ref[0, 1:2, :] = lax.fori_loop(0, n_chunks, bwd_chunk, h0_ref[0, 1:2, :])


def _rnn(za, h0, lw, layer, nb, seq):
    t = za.shape[0]
    rows = 256
    kern = functools.partial(_rnn_kernel, seq=seq, rows=rows)
    return pl.pallas_call(
        kern,
        grid=(nb,),
        in_specs=[pl.BlockSpec((seq, D_RNN), lambda b: (b, 0)),
                  pl.BlockSpec((seq, D_RNN), lambda b: (b, 1)),
                  pl.BlockSpec((1, 2, D_RNN), lambda b: (b, 0, 0)),
                  _layer_spec((CONV_W, D_RNN), layer),
                  _layer_spec((1, D_RNN), layer),
                  _layer_spec((2, D_RNN, 2 * D_RNN), layer),
                  _layer_spec((2, 1, 2 * D_RNN), layer),
                  _layer_spec((2, 1, D_RNN), layer)],
        out_specs=[pl.BlockSpec((seq, D_RNN), lambda b: (b, 0)),
                   pl.BlockSpec((1, 2, D_RNN), lambda b: (b, 0, 0))],
        out_shape=[jax.ShapeDtypeStruct((t, D_RNN), BF16),
                   jax.ShapeDtypeStruct((nb, 2, D_RNN), F32)],
        scratch_shapes=[pltpu.VMEM((seq + 2 * SUBLANES, D_RNN), F32),
                        pltpu.VMEM((seq, D_RNN), F32),
                        pltpu.VMEM((rows, D_RNN), F32)],
        compiler_params=_params("arbitrary"),
        name="rglru",
    )(za, za, h0, lw['conv_w'], lw['conv_b'], lw['wg'], lw['bg'], lw['lam'])


def _attend_t(chains, kv_refs, tk):
    blocks = [(k_ref, vt_ref, j) for k_ref, vt_ref in kv_refs for j in range(vt_ref.shape[0])]

    def scores(block):
        k_ref, _, j = block
        k_blocks = {}
        out = []
        for q, kcols, _ in chains:
            key = (kcols.start, kcols.stop)
            if key not in k_blocks:
                k_blocks[key] = k_ref[j * tk:(j + 1) * tk, kcols]
            out.append(_nt_dot(k_blocks[key], q))
        return out

    state = [None] * len(chains)
    s_cur = scores(blocks[0])
    for n, (_, vt_ref, j) in enumerate(blocks):
        s_next = scores(blocks[n + 1]) if n + 1 < len(blocks) else None
        for c, (_, _, vrows) in enumerate(chains):
            st = s_cur[c]
            m_blk = jnp.max(st, axis=0, keepdims=True)
            if state[c] is None:
                m_new = m_blk
            else:
                m, l, acc = state[c]
                m_new = jnp.maximum(m, m_blk)
                alpha = jnp.exp2(m - m_new)
            p = jnp.exp2(st - m_new)
            l_blk = jnp.sum(p, axis=0, keepdims=True)
            pv = jnp.dot(vt_ref[j, vrows, :], p.astype(BF16), preferred_element_type=F32)
            state[c] = (m_new, l_blk, pv) if state[c] is None else (m_new, alpha * l + l_blk, alpha * acc + pv)
        s_cur = s_next
    return [acc / l for _, l, acc in state]


def _kv_in_specs(kv_parts, nb, tk, k_width, vt_rows):
    specs, args = [], []
    for k, vt in kv_parts:
        n = vt.shape[0] // nb
        specs += [pl.BlockSpec((n * tk, k_width), lambda b, h, i: (b, h)),
                  pl.BlockSpec((n, vt_rows, tk), lambda b, h, i: (b, h, 0))]
        args += [k, vt]
    return specs, args


def _gate_specs(col0, width, tq, row_block):
    w = math.gcd(col0, width) if col0 else width
    return [pl.BlockSpec((tq, w), lambda b, h, i, n=n: (row_block(b, i), (col0 + h * width) // w + n))
            for n in range(width // w)]


def _gate(refs, rows):
    return jnp.concatenate([r[rows, :] for r in refs], axis=1)


def _mla_attn_kernel(q_ref, *refs, tk, n_parts, n_sub, n_heads):
    kv_refs = [(refs[2 * n], refs[2 * n + 1]) for n in range(n_parts)]
    mg_refs, o_ref = refs[2 * n_parts:-1], refs[-1]
    width = min(n_sub, 2) * TQ
    groups = [slice(g * width, (g + 1) * width) for g in range(n_sub * TQ // width)]
    chains = [(q_ref[rows, h * HEAD_PAD:(h + 1) * HEAD_PAD], slice(h * HEAD_PAD, (h + 1) * HEAD_PAD),
               slice(h * MLA_V, (h + 1) * MLA_V)) for rows in groups for h in range(n_heads)]
    outs = _attend_t(chains, kv_refs, tk)
    for g, rows in enumerate(groups):
        ot = jnp.concatenate(outs[g * n_heads:(g + 1) * n_heads], axis=0)
        o_ref[rows, :] = (ot.T * _silu(_gate(mg_refs, rows))).astype(o_ref.dtype)


def _attn_tiling(seq, n_kv, heads):
    if n_kv == 1:
        return 1, heads
    return min(4, seq // TQ), None


def _mla_attn(q, kv_parts, zb, nb, seq, tk):
    t = q.shape[0]
    n_kv = sum(vt.shape[0] for _, vt in kv_parts) // nb
    n_sub, n_heads = _attn_tiling(seq, n_kv, MLA_H)
    n_heads = n_heads or 2
    tq = n_sub * TQ
    nq = seq // tq
    kern = functools.partial(_mla_attn_kernel, tk=tk, n_parts=len(kv_parts), n_sub=n_sub, n_heads=n_heads)
    kv_specs, kv_args = _kv_in_specs(kv_parts, nb, tk, n_heads * HEAD_PAD, n_heads * MLA_V)
    mg_specs = _gate_specs(ZB_MG, n_heads * MLA_V, tq, lambda b, i: b * nq + i)
    return pl.pallas_call(
        kern,
        grid=(nb, MLA_H // n_heads, nq),
        in_specs=[pl.BlockSpec((tq, n_heads * HEAD_PAD), lambda b, h, i: (b * nq + i, h))] + kv_specs + mg_specs,
        out_specs=pl.BlockSpec((tq, n_heads * MLA_V), lambda b, h, i: (b * nq + i, h)),
        out_shape=jax.ShapeDtypeStruct((t, MLA_H * MLA_V), BF16),
        compiler_params=_params("arbitrary", "arbitrary", "arbitrary"),
        name="mla_attn",
    )(q, *kv_args, *([zb] * len(mg_specs)))


def _diff_attn_kernel(q_ref, lam_ref, dn_ref, *refs, tk, n_parts, n_sub, n_heads, lam_init):
    kv_refs = [(refs[2 * n], refs[2 * n + 1]) for n in range(n_parts)]
    dg_refs, o_ref = refs[2 * n_parts:-1], refs[-1]
    lp = lam_ref[...]
    lam = (jnp.exp(jnp.sum(lp[0:1] * lp[1:2], axis=-1, keepdims=True))
           - jnp.exp(jnp.sum(lp[2:3] * lp[3:4], axis=-1, keepdims=True)) + lam_init)
    chains = []
    for s in range(n_sub):
        for h in range(n_heads):
            cols = slice(h * LANES, (h + 1) * LANES)
            q = q_ref[s * TQ:(s + 1) * TQ, cols].astype(F32)
            lane = lax.broadcasted_iota(jnp.int32, q.shape, 1)
            q12 = jnp.concatenate([jnp.where(lane < DIFF_DK, q, 0.0), jnp.where(lane < DIFF_DK, 0.0, q)], axis=0)
            chains.append((q12.astype(BF16), cols, cols))
    outs = _attend_t(chains, kv_refs, tk)
    for s in range(n_sub):
        rows = slice(s * TQ, (s + 1) * TQ)
        heads = []
        for h in range(n_heads):
            o12 = outs[s * n_heads + h]
            od = (o12[:, :TQ] - lam * o12[:, TQ:]).T
            heads.append(_rms(od) * dn_ref[...] * (1.0 - lam_init))
        od = heads[0] if n_heads == 1 else jnp.concatenate(heads, axis=1)
        o_ref[rows, :] = (od * _silu(_gate(dg_refs, rows))).astype(o_ref.dtype)


def _diff_attn(qd, kv_parts, zb, lw, layer, nb, seq, tk):
    t = qd.shape[0]
    n_kv = sum(vt.shape[0] for _, vt in kv_parts) // nb
    n_sub, n_heads = _attn_tiling(seq, n_kv, DIFF_H)
    n_heads = n_heads or 1
    tq = n_sub * TQ
    nq = seq // tq
    lam_init = 0.8 - 0.6 * math.exp(-0.3 * layer)
    kern = functools.partial(_diff_attn_kernel, tk=tk, n_parts=len(kv_parts), n_sub=n_sub, n_heads=n_heads,
                             lam_init=lam_init)
    width = n_heads * LANES
    kv_specs, kv_args = _kv_in_specs(kv_parts, nb, tk, width, width)
    dg_specs = _gate_specs(ZB_DG, width, tq, lambda b, i: b * nq + i)
    return pl.pallas_call(
        kern,
        grid=(nb, DIFF_H // n_heads, nq),
        in_specs=[pl.BlockSpec((tq, width), lambda b, h, i: (b * nq + i, h)),
                  _layer_spec((4, DIFF_DK), layer),
                  _layer_spec((1, DIFF_DV), layer)] + kv_specs + dg_specs,
        out_specs=pl.BlockSpec((tq, width), lambda b, h, i: (b * nq + i, h)),
        out_shape=jax.ShapeDtypeStruct((t, DIFF_COLS), BF16),
        compiler_params=_params("arbitrary", "arbitrary", "arbitrary"),
        name="diff_attn",
    )(qd, lw['lam_p'], lw['diff_norm'], *kv_args, *([zb] * len(dg_specs)))


def _merge_kernel(yr_ref, ym_ref, yd_ref, x_ref, mod_ref, gpre_ref, wg_ref, wr_ref, wm_ref, wd_ref, wo_ref,
                  gpost_ref, o_ref):
    x = x_ref[...]
    h = _modulated_norm(x, gpre_ref, mod_ref).astype(BF16)
    merged = None
    for k, (y_ref, w_ref) in enumerate(((yr_ref, wr_ref), (ym_ref, wm_ref), (yd_ref, wd_ref))):
        gate = _sigmoid(jnp.dot(h, wg_ref[:, k * D_MODEL:(k + 1) * D_MODEL], preferred_element_type=F32))
        term = gate * jnp.dot(y_ref[...], w_ref[...], preferred_element_type=F32)
        merged = term if merged is None else merged + term
    o = jnp.dot(merged.astype(BF16), wo_ref[...], preferred_element_type=F32)
    o_ref[...] = x + mod_ref[0, 2:3, :] * (_rms(o) * gpost_ref[...])


def _merge(y_rnn, y_mla, y_diff, x, mod, lw, layer, rows_per_mod):
    t, d = x.shape
    tm = 512
    tiles_per_mod = rows_per_mod // tm
    row = lambda width: pl.BlockSpec((tm, width), lambda i: (i, 0))
    return pl.pallas_call(
        _merge_kernel,
        grid=(t // tm,),
        in_specs=[row(D_RNN), row(MLA_H * MLA_V), row(DIFF_COLS), row(d),
                  pl.BlockSpec((1, 3, d), lambda i: (i // tiles_per_mod, 0, 0)),
                  _layer_spec((1, d), layer), _layer_spec((d, 3 * d), layer),
                  _layer_spec((D_RNN, d), layer), _layer_spec((MLA_H * MLA_V, d), layer),
                  _layer_spec((DIFF_COLS, d), layer), _layer_spec((d, d), layer), _layer_spec((1, d), layer)],
        out_specs=row(d),
        out_shape=jax.ShapeDtypeStruct((t, d), F32),
        compiler_params=_params("arbitrary"),
        name="merge_out",
    )(y_rnn, y_mla, y_diff, x, mod, lw['g_pre'], lw['w_mgate'], lw['w_br_rnn'], lw['w_br_mla'], lw['w_br_diff'],
      lw['w_out'], lw['g_post'])


def _pad_heads(w, width):
    l, k, _ = w.shape
    w4 = jnp.pad(w.reshape(l, k, MLA_H, width), ((0, 0), (0, 0), (0, 0), (0, HEAD_PAD - width)))
    return w4.reshape(l, k, MLA_H * HEAD_PAD)


def _layer_params(p):
    depth, d, _ = p['w_in'].shape
    w_in = p['w_in']
    kr_pad = ((0, 0), (0, 0), (MLA_NOPE, HEAD_PAD - MLA_NOPE - MLA_ROPE))
    eye = jnp.eye(RNN_BLOCKS, dtype=F32)[None, None, :, None, :, None]
    dense = lambda w: (w[:, :, :, :, None, :] * eye).reshape(depth, 2, D_RNN, D_RNN)
    ukv = p['w_ukv'].reshape(depth, KV_RANK, MLA_H, MLA_NOPE + MLA_V)
    return dict(
        g_pre=p['g_pre'].reshape(depth, 1, d), g_post=p['g_post'].reshape(depth, 1, d),
        w_a=w_in[:, :, :_O_KR].astype(BF16), w_krp=jnp.pad(w_in[:, :, _O_KR:_O_MG], kr_pad).astype(BF16),
        w_b=w_in[:, :, _O_MG:_O_MGATE].astype(BF16), w_mgate=w_in[:, :, _O_MGATE:].astype(BF16),
        conv_w=p['conv_w'], conv_b=p['conv_b'].reshape(depth, 1, D_RNN),
        wg=(0.5 * jnp.concatenate([dense(p['w_rg_a']), dense(p['w_rg_x'])], axis=-1)).astype(BF16),
        bg=0.5 * jnp.concatenate([p['b_rg_a'], p['b_rg_x']], axis=-1).reshape(depth, 2, 1, 2 * D_RNN),
        lam=p['rg_lam'].reshape(depth, 2, 1, D_RNN),
        q_norm=p['q_norm'].reshape(depth, 1, Q_RANK), kv_norm=p['kv_norm'].reshape(depth, 1, KV_RANK),
        wuq=_pad_heads(p['w_uq'], MLA_NOPE + MLA_ROPE).astype(BF16),
        wuk=_pad_heads(ukv[..., :MLA_NOPE].reshape(depth, KV_RANK, MLA_H * MLA_NOPE), MLA_NOPE).astype(BF16),
        wuvt=jnp.swapaxes(ukv[..., MLA_NOPE:].reshape(depth, KV_RANK, MLA_H * MLA_V), 1, 2).astype(BF16),
        lam_p=jnp.stack([p['lam_q1'], p['lam_k1'], p['lam_q2'], p['lam_k2']], axis=1),
        diff_norm=p['diff_norm'].reshape(depth, 1, DIFF_DV),
        w_br_rnn=p['w_br_rnn'].astype(BF16), w_br_mla=p['w_br_mla'].astype(BF16),
        w_br_diff=p['w_br_diff'].astype(BF16), w_out=p['w_out'].astype(BF16))


def _rope_tables(n_pos, groups, dim):
    half = dim // 4
    inv = ROPE_THETA ** (-np.arange(half, dtype=np.float32) / half)
    inv_row = np.zeros((LANES,), np.float32)
    inv_col = np.zeros((LANES,), np.float32)
    first = np.zeros((LANES,), np.float32)
    second = np.zeros((LANES,), np.float32)
    for off in groups:
        for inv_axis, base in ((inv_row, off), (inv_col, off + 2 * half)):
            inv_axis[base:base + half] = inv
            inv_axis[base + half:base + 2 * half] = inv
            first[base:base + half] = 1.0
            second[base + half:base + 2 * half] = 1.0
    pos = jnp.arange(n_pos, dtype=jnp.int32)
    ang = ((pos // GRID_W).astype(F32)[:, None] * jnp.asarray(inv_row)[None, :]
           + (pos % GRID_W).astype(F32)[:, None] * jnp.asarray(inv_col)[None, :])
    sin = jnp.sin(ang)
    return jnp.stack([jnp.cos(ang), -sin * jnp.asarray(first), sin * jnp.asarray(second)])


def _layer(l, x, mod, lw, nb, seq, rows_per_mod, h0, tabs, ctx, prev_caches=()):
    tk = min(TK, seq)
    zs, q, ckvn, qd, (k, vt, kd, vdt) = _inproj(x, mod, tabs, lw, l, seq, rows_per_mod, tk, ctx is None,
                                                 prev_caches)
    y_rnn, st = _rnn(zs[0], h0, lw, l, nb, seq)
    mla_kv, diff_kv = [(k, vt)], [(kd, vdt)]
    if ctx is not None:
        kc, vtc, kdc, vdtc = _ctx(*ctx, lw, l, tk)
        mla_kv, diff_kv = [(kc, vtc)] + mla_kv, [(kdc, vdtc)] + diff_kv
    y_mla = _mla_attn(q, mla_kv, zs[1], nb, seq, tk)
    y_diff = _diff_attn(qd, diff_kv, zs[1], lw, l, nb, seq, tk)
    x_new = _merge(y_rnn, y_mla, y_diff, x, mod, lw, l, rows_per_mod)
    return x_new, zs[2:], ckvn, st


def kernel(x_prompt, x_sample, cache_mla_ckv, cache_mla_krope, cache_diff_k, cache_diff_v, state_rnn, c, c_ctx, w_mod, b_mod, g_pre, g_post, w_in, conv_w, conv_b, w_rg_a, b_rg_a, w_rg_x, b_rg_x, rg_lam, q_norm, w_uq, kv_norm, w_ukv, lam_q1, lam_k1, lam_q2, lam_k2, diff_norm, w_br_rnn, w_br_mla, w_br_diff, w_out):
    depth = w_in.shape[0]
    nbp, sp, d = x_prompt.shape
    nbs, ss, _ = x_sample.shape

    n_cond = -(-(1 + nbs) // SUBLANES) * SUBLANES
    cond = jnp.concatenate([c_ctx[None, :], c, jnp.zeros((n_cond - 1 - nbs, d), F32)], axis=0)
    mod = _modulation(cond, w_mod, b_mod).reshape(depth, n_cond, 3, d)

    lw = _layer_params(dict(
        w_in=w_in, g_pre=g_pre, g_post=g_post, conv_w=conv_w, conv_b=conv_b, w_rg_a=w_rg_a, b_rg_a=b_rg_a,
        w_rg_x=w_rg_x, b_rg_x=b_rg_x, rg_lam=rg_lam, q_norm=q_norm, w_uq=w_uq, kv_norm=kv_norm, w_ukv=w_ukv,
        lam_q1=lam_q1, lam_k1=lam_k1, lam_q2=lam_q2, lam_k2=lam_k2, diff_norm=diff_norm, w_br_rnn=w_br_rnn,
        w_br_mla=w_br_mla, w_br_diff=w_br_diff, w_out=w_out))

    xp = x_prompt.reshape(nbp * sp, d)
    h0 = jnp.zeros((nbp, 2, D_RNN), F32)
    caches, st_l = [], []
    for l in range(depth):
        xp, (krc, dkc, dvc), ckvn, st = _layer(l, xp, mod[l, 0:1], lw, nbp, sp, nbp * sp, h0, None, None,
                                               caches if l == depth - 1 else ())
        caches.append((krc, dkc, dvc, ckvn))
        st_l.append(st)
    new_kr, new_dk, new_dv, new_ckv = caches[-1]

    tabs = (_rope_tables(ss, (MLA_NOPE,), MLA_ROPE), _rope_tables(ss, (0, DIFF_DK), DIFF_DK))
    ctx = (cache_mla_ckv, cache_mla_krope, cache_diff_k, cache_diff_v)
    xs = x_sample.reshape(nbs * ss, d)
    for l in range(depth):
        xs, _, _, _ = _layer(l, xs, mod[l, 1:1 + nbs], lw, nbs, ss, ss, state_rnn[:, l], tabs, ctx)

    return (xp.reshape(nbp, sp, d), xs.reshape(nbs, ss, d), new_ckv, new_kr, new_dk, new_dv, jnp.stack(st_l, axis=1))
```

```python
import functools
import math

import jax
import jax.numpy as jnp
import numpy as np
from jax import lax
from jax.experimental import pallas as pl
from jax.experimental.pallas import tpu as pltpu

F32 = jnp.float32
BF16 = jnp.bfloat16

D_MODEL = 1024
GRID_W = 64
EPS = 1e-6
ROPE_THETA = 10000.0
LOG2E = math.log2(math.e)

D_RNN = 512
RNN_BLOCKS = 8
CONV_W = 4
RG_C = 8.0

MLA_H = 8
MLA_NOPE = 64
MLA_ROPE = 32
MLA_V = 64
Q_RANK = 384
KV_RANK = 256
MLA_SCALE = (MLA_NOPE + MLA_ROPE) ** -0.5

DIFF_H = 4
DIFF_DK = 64
DIFF_DV = 2 * DIFF_DK
DIFF_SCALE = DIFF_DK ** -0.5
DIFF_COLS = DIFF_H * DIFF_DV

LANES = 128
SUBLANES = 8
HEAD_PAD = LANES
TQ = 256
TK = 512
VMEM_LIMIT = 56 * 1024 * 1024

_O_KR, _O_MG, _O_MGATE = 1664, 1696, 4256
A_RNN, A_LAT, A_COLS = 0, 1024, 1664
B_MG, B_QKV, B_DG, B_COLS = 0, 512, 2048, 2560
ZB_MG = 0
ZB_DG = 512


def _sigmoid(x):
    return 0.5 * jnp.tanh(0.5 * x) + 0.5


def _silu(x):
    return x * _sigmoid(x)


def _rms(x):
    return x * lax.rsqrt(jnp.mean(x * x, axis=-1, keepdims=True) + EPS)


def _params(*sem):
    return pltpu.CompilerParams(dimension_semantics=sem, vmem_limit_bytes=VMEM_LIMIT)


def _layer_spec(shape, layer):
    return pl.BlockSpec((None,) + tuple(shape), lambda *_: (layer,) + (0,) * len(shape))


def _mod_kernel(cond_ref, w_ref, b_ref, o_ref):
    c = cond_ref[...]
    o_ref[0] = jnp.dot(_silu(c).astype(BF16), w_ref[0].astype(BF16), preferred_element_type=F32) + b_ref[0]


def _modulation(cond, w_mod, b_mod):
    depth, d, n = w_mod.shape
    r = cond.shape[0]
    tn = 1024
    return pl.pallas_call(
        _mod_kernel,
        grid=(depth, n // tn),
        in_specs=[pl.BlockSpec((r, d), lambda l, j: (0, 0)),
                  pl.BlockSpec((1, d, tn), lambda l, j: (l, 0, j)),
                  pl.BlockSpec((1, 1, tn), lambda l, j: (l, 0, j))],
        out_specs=pl.BlockSpec((1, r, tn), lambda l, j: (l, 0, j)),
        out_shape=jax.ShapeDtypeStruct((depth, r, n), F32),
        compiler_params=_params("arbitrary", "arbitrary"),
        name="modulation",
    )(cond, w_mod, b_mod.reshape(depth, 1, n))


def _modulated_norm(x, g_ref, mod_ref):
    return (_rms(x) * g_ref[...]) * (1.0 + mod_ref[0, 1:2, :]) + mod_ref[0, 0:1, :]


def _rope(x, tab_ref, half):
    n = x.shape[-1]
    return x * tab_ref[0] + pltpu.roll(x, n - half, 1) * tab_ref[1] + pltpu.roll(x, half, 1) * tab_ref[2]


def _nt_dot(a, b):
    return lax.dot_general(a, b, (((1,), (1,)), ((), ())), preferred_element_type=F32)


def _kv_up(ckvn, krp, wuk_ref, wuvt_ref, k_ref, vt_ref):
    cb = ckvn.astype(BF16)
    kn = jnp.dot(cb, wuk_ref[...], preferred_element_type=F32)
    vt_ref[0] = _nt_dot(wuvt_ref[...], cb).astype(BF16)
    for h in range(MLA_H):
        sl = slice(h * HEAD_PAD, (h + 1) * HEAD_PAD)
        k_ref[:, sl] = (kn[:, sl] + krp).astype(BF16)


def _inproj_kernel(*refs, rope, caches, n_prev):
    refs = list(refs)
    x_ref, mod_ref, g_ref, wa_ref, wkr_ref, wb_ref = refs[:6]
    tabm_ref, tabd_ref = (refs[6], refs[7]) if rope else (None, None)
    n_in = (13 if rope else 11) + 4 * n_prev
    qn_ref, wuq_ref, kvn_ref, wuk_ref, wuvt_ref = refs[n_in - 4 * n_prev - 5:n_in - 4 * n_prev]
    prev = refs[n_in - 4 * n_prev:n_in]
    outs = refs[n_in:]
    za_ref, zb_ref = outs[:2]
    krc_ref, dkc_ref, dvc_ref = outs[2:5] if caches else (None, None, None)
    q_ref, k_ref, vt_ref, ckvn_ref, qd_ref, kd_ref, vdt_ref = outs[5:] if caches else outs[2:]
    if n_prev:
        for l in range(n_prev):
            for dst, src in zip((krc_ref, dkc_ref, dvc_ref, ckvn_ref), prev[4 * l:4 * l + 4]):
                dst[0, l] = src[...]
        krc_ref, dkc_ref, dvc_ref, ckvn_ref = (r.at[0, n_prev] for r in (krc_ref, dkc_ref, dvc_ref, ckvn_ref))

    h = _modulated_norm(x_ref[...], g_ref, mod_ref).astype(BF16)
    proj = lambda w_ref, c0, c1: jnp.dot(h, w_ref[:, c0:c1], preferred_element_type=F32)
    za_ref[...] = proj(wa_ref, A_RNN, A_LAT)

    lat = proj(wa_ref, A_LAT, A_COLS)
    cqn = _rms(lat[:, 0:Q_RANK]) * qn_ref[...]
    q = jnp.dot(cqn.astype(BF16), wuq_ref[...], preferred_element_type=F32) * (MLA_SCALE * LOG2E)
    ckvn = _rms(lat[:, Q_RANK:Q_RANK + KV_RANK]) * kvn_ref[...]
    ckvn_ref[...] = ckvn
    krp = jnp.dot(h, wkr_ref[...], preferred_element_type=F32)
    if caches:
        krc_ref[...] = krp[:, MLA_NOPE:MLA_NOPE + MLA_ROPE]
    if rope:
        krp = _rope(krp, tabm_ref, MLA_ROPE // 4)
    _kv_up(ckvn, krp, wuk_ref, wuvt_ref, k_ref, vt_ref)
    for hd in range(MLA_H):
        sl = slice(hd * HEAD_PAD, (hd + 1) * HEAD_PAD)
        qh = q[:, sl]
        if rope:
            qh = _rope(qh, tabm_ref, MLA_ROPE // 4)
        q_ref[:, sl] = qh.astype(BF16)
    zb_ref[:, ZB_MG:ZB_MG + MLA_H * MLA_V] = proj(wb_ref, B_MG, B_QKV)

    qkv = proj(wb_ref, B_QKV, B_DG)
    for hd in range(DIFF_H):
        sl = slice(hd * LANES, (hd + 1) * LANES)
        qd = qkv[:, sl]
        kd = qkv[:, DIFF_COLS + hd * LANES:DIFF_COLS + (hd + 1) * LANES]
        vd = qkv[:, 2 * DIFF_COLS + hd * LANES:2 * DIFF_COLS + (hd + 1) * LANES]
        if caches:
            dkc_ref[:, hd, :] = kd
            dvc_ref[:, hd, :] = vd
        if rope:
            qd = _rope(qd, tabd_ref, DIFF_DK // 4)
            kd = _rope(kd, tabd_ref, DIFF_DK // 4)
        qd_ref[:, sl] = (qd * (DIFF_SCALE * LOG2E)).astype(BF16)
        kd_ref[:, sl] = kd.astype(BF16)
        vdt_ref[0, sl, :] = vd.T.astype(BF16)
    zb_ref[:, ZB_DG:ZB_DG + DIFF_COLS] = proj(wb_ref, B_DG, B_COLS)


def _kv_specs(tk):
    rows = lambda width: pl.BlockSpec((tk, width), lambda i: (i, 0))
    tr = lambda height: pl.BlockSpec((1, height, tk), lambda i: (i, 0, 0))
    return [rows(MLA_H * HEAD_PAD), tr(MLA_H * MLA_V), rows(DIFF_COLS), tr(DIFF_COLS)]


def _kv_shapes(n_blocks, tk):
    return [jax.ShapeDtypeStruct((n_blocks * tk, MLA_H * HEAD_PAD), BF16),
            jax.ShapeDtypeStruct((n_blocks, MLA_H * MLA_V, tk), BF16),
            jax.ShapeDtypeStruct((n_blocks * tk, DIFF_COLS), BF16),
            jax.ShapeDtypeStruct((n_blocks, DIFF_COLS, tk), BF16)]


def _inproj(x, mod, tabs, lw, layer, seq, rows_per_mod, tk, caches, prev_caches=()):
    t, d = x.shape
    tm = tk
    tiles_per_mod = rows_per_mod // tm
    rope = tabs is not None
    in_specs = [pl.BlockSpec((tm, d), lambda i: (i, 0)),
                pl.BlockSpec((1, 3, d), lambda i: (i // tiles_per_mod, 0, 0)),
                _layer_spec((1, d), layer),
                _layer_spec((d, A_COLS), layer), _layer_spec((d, LANES), layer), _layer_spec((d, B_COLS), layer)]
    args = [x, mod, lw['g_pre'], lw['w_a'], lw['w_krp'], lw['w_b']]
    if rope:
        tab_spec = pl.BlockSpec((3, tm, LANES), lambda i: (0, i % (seq // tm), 0))
        in_specs += [tab_spec, tab_spec]
        args += [tabs[0], tabs[1]]
    in_specs += [_layer_spec((1, Q_RANK), layer), _layer_spec((Q_RANK, MLA_H * HEAD_PAD), layer),
                 _layer_spec((1, KV_RANK), layer), _layer_spec((KV_RANK, MLA_H * HEAD_PAD), layer),
                 _layer_spec((MLA_H * MLA_V, KV_RANK), layer)]
    args += [lw['q_norm'], lw['wuq'], lw['kv_norm'], lw['wuk'], lw['wuvt']]
    row = lambda width: pl.BlockSpec((tm, width), lambda i: (i, 0))
    heads = pl.BlockSpec((tm, DIFF_H, DIFF_DV), lambda i: (i, 0, 0))
    n_prev = len(prev_caches)
    for prev in prev_caches:
        in_specs += [row(MLA_ROPE), heads, heads, row(KV_RANK)]
        args += list(prev)
    f32 = lambda width: jax.ShapeDtypeStruct((t, width), F32)
    bf16 = lambda width: jax.ShapeDtypeStruct((t, width), BF16)
    k_spec, vt_spec, kd_spec, vdt_spec = _kv_specs(tk)
    k_shape, vt_shape, kd_shape, vdt_shape = _kv_shapes(t // tk, tk)
    qk = MLA_H * HEAD_PAD
    z_specs = [row(A_LAT), row(ZB_DG + DIFF_COLS)]
    z_shapes = [f32(A_LAT), f32(ZB_DG + DIFF_COLS)]
    ckvn_spec, ckvn_shape = row(KV_RANK), f32(KV_RANK)
    if caches and not n_prev:
        z_specs += [row(MLA_ROPE), heads, heads]
        z_shapes += [f32(MLA_ROPE)] + [jax.ShapeDtypeStruct((t, DIFF_H, DIFF_DV), F32)] * 2
    elif caches:
        per = seq // tm

        def stacked(*minor):
            spec = pl.BlockSpec((1, n_prev + 1, tm) + minor, lambda i: (i // per, 0, i % per) + (0,) * len(minor))
            return spec, jax.ShapeDtypeStruct((t // seq, n_prev + 1, seq) + minor, F32)

        (kr_spec, kr_shape), (hd_spec, hd_shape) = stacked(MLA_ROPE), stacked(DIFF_H, DIFF_DV)
        ckvn_spec, ckvn_shape = stacked(KV_RANK)
        z_specs += [kr_spec, hd_spec, hd_spec]
        z_shapes += [kr_shape, hd_shape, hd_shape]
    outs = pl.pallas_call(
        functools.partial(_inproj_kernel, rope=rope, caches=caches, n_prev=n_prev),
        grid=(t // tm,),
        in_specs=in_specs,
        out_specs=z_specs + [row(qk), k_spec, vt_spec, ckvn_spec, row(DIFF_COLS), kd_spec, vdt_spec],
        out_shape=z_shapes + [bf16(qk), k_shape, vt_shape, ckvn_shape, bf16(DIFF_COLS), kd_shape, vdt_shape],
        compiler_params=_params("arbitrary"),
        name="inproj",
    )(*args)
    zs, (q, k, vt, ckvn, qd, kd, vdt) = outs[:len(z_specs)], outs[len(z_specs):]
    return zs, q, ckvn, qd, (k, vt, kd, vdt)


def _ctx_kernel(ckvn_ref, kr_ref, dk_ref, dv_ref, wuk_ref, wuvt_ref, k_ref, vt_ref, kd_ref, vdt_ref):
    row = lax.broadcasted_iota(jnp.int32, (MLA_ROPE, LANES), 0)
    lane = lax.broadcasted_iota(jnp.int32, (MLA_ROPE, LANES), 1)
    place = (lane == row + MLA_NOPE).astype(BF16)
    krp = jnp.dot(kr_ref[...].astype(BF16), place, preferred_element_type=F32)
    _kv_up(ckvn_ref[...], krp, wuk_ref, wuvt_ref, k_ref, vt_ref)
    for h in range(DIFF_H):
        sl = slice(h * LANES, (h + 1) * LANES)
        kd_ref[:, sl] = dk_ref[:, h, :].astype(BF16)
        vdt_ref[0, sl, :] = dv_ref[:, h, :].T.astype(BF16)


def _ctx(ckvn, kr, dk, dv, lw, layer, tk):
    nb, _, past, _ = ckvn.shape
    n_past = past // tk
    cached = lambda *minor: pl.BlockSpec((None, None, tk) + minor,
                                         lambda i: (i // n_past, layer, i % n_past) + (0,) * len(minor))
    return pl.pallas_call(
        _ctx_kernel,
        grid=(nb * n_past,),
        in_specs=[cached(KV_RANK), cached(MLA_ROPE), cached(DIFF_H, DIFF_DV), cached(DIFF_H, DIFF_DV),
                  _layer_spec((KV_RANK, MLA_H * HEAD_PAD), layer), _layer_spec((MLA_H * MLA_V, KV_RANK), layer)],
        out_specs=_kv_specs(tk),
        out_shape=_kv_shapes(nb * n_past, tk),
        compiler_params=_params("arbitrary"),
        name="ctx_kv",
    )(ckvn, kr, dk, dv, lw['wuk'], lw['wuvt'])


def _tile_scan(a, b, reverse):
    r_in_tile = lax.broadcasted_iota(jnp.int32, a.shape, 1)
    for k in (1, 2, 4):
        shift = SUBLANES - k if reverse else k
        valid = (r_in_tile < SUBLANES - k) if reverse else (r_in_tile >= k)
        a_s = jnp.where(valid, pltpu.roll(a, shift, 1), 1.0)
        b_s = jnp.where(valid, pltpu.roll(b, shift, 1), 0.0)
        b = a * b_s + b
        a = a * a_s
    return a, b


def _rnn_kernel(rx_ref, rg_ref, h0_ref, cw_ref, cb_ref, wg_ref, bg_ref, lam_ref, y_ref, st_ref,
                xpad, hf, hb, *, seq, rows):
    n_chunks = seq // rows
    n_tiles = rows // SUBLANES
    pad = SUBLANES

    xpad[0:pad, :] = jnp.zeros((pad, D_RNN), F32)
    xpad[seq + pad:seq + 2 * pad, :] = jnp.zeros((pad, D_RNN), F32)

    def copy_chunk(c, carry):
        r0 = pl.multiple_of(c * rows, rows)
        xpad[pl.ds(r0 + pad, rows), :] = rx_ref[pl.ds(r0, rows), :]
        return carry

    lax.fori_loop(0, n_chunks, copy_chunk, 0)

    def chunk_maps(r0, d):
        n = rows + 2 * pad
        xe = xpad[pl.ds(r0, n), :]
        taps = (pltpu.roll(xe, 2, 0), pltpu.roll(xe, 1, 0), xe, pltpu.roll(xe, n - 1, 0))
        xc = cb_ref[...]
        for k in range(CONV_W):
            xc = xc + taps[k][pad:pad + rows] * cw_ref[k:k + 1, :]
        t = jnp.tanh(jnp.dot(xc.astype(BF16), wg_ref[d], preferred_element_type=F32) + bg_ref[d])
        i = 0.5 * t[:, D_RNN:] + 0.5
        neg_lam = -lam_ref[d]
        softplus = jnp.maximum(neg_lam, 0.0) + jnp.log1p(jnp.exp(-jnp.abs(neg_lam)))
        c = (-0.5 * RG_C) * softplus
        log_a = c * t[:, :D_RNN] + c
        a = jnp.exp(log_a)
        b = jnp.sqrt(-jnp.tanh(log_a) * (1.0 + a * a)) * (i * xc)
        return a.reshape(n_tiles, SUBLANES, D_RNN), b.reshape(n_tiles, SUBLANES, D_RNN)

    def fwd_chunk(c, carry):
        r0 = pl.multiple_of(c * rows, rows)
        a3, b3 = _tile_scan(*chunk_maps(r0, 0), reverse=False)
        for j in range(n_tiles):
            ht = a3[j] * carry + b3[j]
            hf[pl.ds(r0 + j * SUBLANES, SUBLANES), :] = ht
            carry = ht[SUBLANES - 1:SUBLANES, :]
        return carry

    def bwd_chunk(cc, carry):
        r0 = pl.multiple_of((n_chunks - 1 - cc) * rows, rows)
        a3, b3 = _tile_scan(*chunk_maps(r0, 1), reverse=True)
        for j in reversed(range(n_tiles)):
            ht = a3[j] * carry + b3[j]
            hb[j * SUBLANES:(j + 1) * SUBLANES, :] = ht
            carry = ht[0:1, :]
        y = (hf[pl.ds(r0, rows), :] + hb[...]) * _silu(rg_ref[pl.ds(r0, rows), :])
        y_ref[pl.ds(r0, rows), :] = y.astype(y_ref.dtype)
        return carry

    st_ref[0, 0:1, :] = lax.fori_loop(0, n_chunks, fwd_chunk, h0_ref[0, 0:1, :])
    st_ref[0, 1:2, :] = lax.fori_loop(0, n_chunks, bwd_chunk, h0_ref[0, 1:2, :])


def _rnn(za, h0, lw, layer, nb, seq):
    t = za.shape[0]
    rows = 256
    kern = functools.partial(_rnn_kernel, seq=seq, rows=rows)
    return pl.pallas_call(
        kern,
        grid=(nb,),
        in_specs=[pl.BlockSpec((seq, D_RNN), lambda b: (b, 0)),
                  pl.BlockSpec((seq, D_RNN), lambda b: (b, 1)),
                  pl.BlockSpec((1, 2, D_RNN), lambda b: (b, 0, 0)),
                  _layer_spec((CONV_W, D_RNN), layer),
                  _layer_spec((1, D_RNN), layer),
                  _layer_spec((2, D_RNN, 2 * D_RNN), layer),
                  _layer_spec((2, 1, 2 * D_RNN), layer),
                  _layer_spec((2, 1, D_RNN), layer)],
        out_specs=[pl.BlockSpec((seq, D_RNN), lambda b: (b, 0)),
                   pl.BlockSpec((1, 2, D_RNN), lambda b: (b, 0, 0))],
        out_shape=[jax.ShapeDtypeStruct((t, D_RNN), BF16),
                   jax.ShapeDtypeStruct((nb, 2, D_RNN), F32)],
        scratch_shapes=[pltpu.VMEM((seq + 2 * SUBLANES, D_RNN), F32),
                        pltpu.VMEM((seq, D_RNN), F32),
                        pltpu.VMEM((rows, D_RNN), F32)],
        compiler_params=_params("arbitrary"),
        name="rglru",
    )(za, za, h0, lw['conv_w'], lw['conv_b'], lw['wg'], lw['bg'], lw['lam'])


def _attend_t(chains, kv_refs, tk):
    blocks = [(k_ref, vt_ref, j) for k_ref, vt_ref in kv_refs for j in range(vt_ref.shape[0])]

    def scores(block):
        k_ref, _, j = block
        k_blocks = {}
        out = []
        for q, kcols, _ in chains:
            key = (kcols.start, kcols.stop)
            if key not in k_blocks:
                k_blocks[key] = k_ref[j * tk:(j + 1) * tk, kcols]
            out.append(_nt_dot(k_blocks[key], q))
        return out

    state = [None] * len(chains)
    s_cur = scores(blocks[0])
    for n, (_, vt_ref, j) in enumerate(blocks):
        s_next = scores(blocks[n + 1]) if n + 1 < len(blocks) else None
        for c, (_, _, vrows) in enumerate(chains):
            st = s_cur[c]
            m_blk = jnp.max(st, axis=0, keepdims=True)
            if state[c] is None:
                m_new = m_blk
            else:
                m, l, acc = state[c]
                m_new = jnp.maximum(m, m_blk)
                alpha = jnp.exp2(m - m_new)
            p = jnp.exp2(st - m_new)
            l_blk = jnp.sum(p, axis=0, keepdims=True)
            pv = jnp.dot(vt_ref[j, vrows, :], p.astype(BF16), preferred_element_type=F32)
            state[c] = (m_new, l_blk, pv) if state[c] is None else (m_new, alpha * l + l_blk, alpha * acc + pv)
        s_cur = s_next
    return [acc / l for _, l, acc in state]


def _kv_in_specs(kv_parts, nb, tk, k_width, vt_rows):
    specs, args = [], []
    for k, vt in kv_parts:
        n = vt.shape[0] // nb
        specs += [pl.BlockSpec((n * tk, k_width), lambda b, h, i: (b, h)),
                  pl.BlockSpec((n, vt_rows, tk), lambda b, h, i: (b, h, 0))]
        args += [k, vt]
    return specs, args


def _gate_specs(col0, width, tq, row_block):
    w = math.gcd(col0, width) if col0 else width
    return [pl.BlockSpec((tq, w), lambda b, h, i, n=n: (row_block(b, i), (col0 + h * width) // w + n))
            for n in range(width // w)]


def _gate(refs, rows):
    return jnp.concatenate([r[rows, :] for r in refs], axis=1)


def _mla_attn_kernel(q_ref, *refs, tk, n_parts, n_sub, n_heads):
    kv_refs = [(refs[2 * n], refs[2 * n + 1]) for n in range(n_parts)]
    mg_refs, o_ref = refs[2 * n_parts:-1], refs[-1]
    width = min(n_sub, 2) * TQ
    groups = [slice(g * width, (g + 1) * width) for g in range(n_sub * TQ // width)]
    chains = [(q_ref[rows, h * HEAD_PAD:(h + 1) * HEAD_PAD], slice(h * HEAD_PAD, (h + 1) * HEAD_PAD),
               slice(h * MLA_V, (h + 1) * MLA_V)) for rows in groups for h in range(n_heads)]
    outs = _attend_t(chains, kv_refs, tk)
    for g, rows in enumerate(groups):
        ot = jnp.concatenate(outs[g * n_heads:(g + 1) * n_heads], axis=0)
        o_ref[rows, :] = (ot.T * _silu(_gate(mg_refs, rows))).astype(o_ref.dtype)


def _attn_tiling(seq, n_kv, heads):
    if n_kv == 1:
        return 1, heads
    return min(4, seq // TQ), None


def _mla_attn(q, kv_parts, zb, nb, seq, tk):
    t = q.shape[0]
    n_kv = sum(vt.shape[0] for _, vt in kv_parts) // nb
    n_sub, n_heads = _attn_tiling(seq, n_kv, MLA_H)
    n_heads = n_heads or 2
    tq = n_sub * TQ
    nq = seq // tq
    kern = functools.partial(_mla_attn_kernel, tk=tk, n_parts=len(kv_parts), n_sub=n_sub, n_heads=n_heads)
    kv_specs, kv_args = _kv_in_specs(kv_parts, nb, tk, n_heads * HEAD_PAD, n_heads * MLA_V)
    mg_specs = _gate_specs(ZB_MG, n_heads * MLA_V, tq, lambda b, i: b * nq + i)
    return pl.pallas_call(
        kern,
        grid=(nb, MLA_H // n_heads, nq),
        in_specs=[pl.BlockSpec((tq, n_heads * HEAD_PAD), lambda b, h, i: (b * nq + i, h))] + kv_specs + mg_specs,
        out_specs=pl.BlockSpec((tq, n_heads * MLA_V), lambda b, h, i: (b * nq + i, h)),
        out_shape=jax.ShapeDtypeStruct((t, MLA_H * MLA_V), BF16),
        compiler_params=_params("arbitrary", "arbitrary", "arbitrary"),
        name="mla_attn",
    )(q, *kv_args, *([zb] * len(mg_specs)))


def _diff_attn_kernel(q_ref, lam_ref, dn_ref, *refs, tk, n_parts, n_sub, n_heads, lam_init):
    kv_refs = [(refs[2 * n], refs[2 * n + 1]) for n in range(n_parts)]
    dg_refs, o_ref = refs[2 * n_parts:-1], refs[-1]
    lp = lam_ref[...]
    lam = (jnp.exp(jnp.sum(lp[0:1] * lp[1:2], axis=-1, keepdims=True))
           - jnp.exp(jnp.sum(lp[2:3] * lp[3:4], axis=-1, keepdims=True)) + lam_init)
    chains = []
    for s in range(n_sub):
        for h in range(n_heads):
            cols = slice(h * LANES, (h + 1) * LANES)
            q = q_ref[s * TQ:(s + 1) * TQ, cols].astype(F32)
            lane = lax.broadcasted_iota(jnp.int32, q.shape, 1)
            q12 = jnp.concatenate([jnp.where(lane < DIFF_DK, q, 0.0), jnp.where(lane < DIFF_DK, 0.0, q)], axis=0)
            chains.append((q12.astype(BF16), cols, cols))
    outs = _attend_t(chains, kv_refs, tk)
    for s in range(n_sub):
        rows = slice(s * TQ, (s + 1) * TQ)
        heads = []
        for h in range(n_heads):
            o12 = outs[s * n_heads + h]
            od = (o12[:, :TQ] - lam * o12[:, TQ:]).T
            heads.append(_rms(od) * dn_ref[...] * (1.0 - lam_init))
        od = heads[0] if n_heads == 1 else jnp.concatenate(heads, axis=1)
        o_ref[rows, :] = (od * _silu(_gate(dg_refs, rows))).astype(o_ref.dtype)


def _diff_attn(qd, kv_parts, zb, lw, layer, nb, seq, tk):
    t = qd.shape[0]
    n_kv = sum(vt.shape[0] for _, vt in kv_parts) // nb
    n_sub, n_heads = _attn_tiling(seq, n_kv, DIFF_H)
    n_heads = n_heads or 1
    tq = n_sub * TQ
    nq = seq // tq
    lam_init = 0.8 - 0.6 * math.exp(-0.3 * layer)
    kern = functools.partial(_diff_attn_kernel, tk=tk, n_parts=len(kv_parts), n_sub=n_sub, n_heads=n_heads,
                             lam_init=lam_init)
    width = n_heads * LANES
    kv_specs, kv_args = _kv_in_specs(kv_parts, nb, tk, width, width)
    dg_specs = _gate_specs(ZB_DG, width, tq, lambda b, i: b * nq + i)
    return pl.pallas_call(
        kern,
        grid=(nb, DIFF_H // n_heads, nq),
        in_specs=[pl.BlockSpec((tq, width), lambda b, h, i: (b * nq + i, h)),
                  _layer_spec((4, DIFF_DK), layer),
                  _layer_spec((1, DIFF_DV), layer)] + kv_specs + dg_specs,
        out_specs=pl.BlockSpec((tq, width), lambda b, h, i: (b * nq + i, h)),
        out_shape=jax.ShapeDtypeStruct((t, DIFF_COLS), BF16),
        compiler_params=_params("arbitrary", "arbitrary", "arbitrary"),
        name="diff_attn",
    )(qd, lw['lam_p'], lw['diff_norm'], *kv_args, *([zb] * len(dg_specs)))


def _block_attn_kernel(q_ref, k_ref, vt_ref, qd_ref, kd_ref, vdt_ref, lam_ref, dn_ref, zb_ref, om_ref, od_ref, *,
                       tk, lam_init):
    _mla_attn_kernel(q_ref, k_ref, vt_ref, zb_ref.at[:, ZB_MG:ZB_MG + MLA_H * MLA_V], om_ref,
                     tk=tk, n_parts=1, n_sub=1, n_heads=MLA_H)
    _diff_attn_kernel(qd_ref, lam_ref, dn_ref, kd_ref, vdt_ref, zb_ref.at[:, ZB_DG:ZB_DG + DIFF_COLS], od_ref,
                      tk=tk, n_parts=1, n_sub=1, n_heads=DIFF_H, lam_init=lam_init)


def _block_attn(q, k, vt, qd, kd, vdt, zb, lw, layer, nb, seq):
    t = q.shape[0]
    lam_init = 0.8 - 0.6 * math.exp(-0.3 * layer)
    row = lambda width: pl.BlockSpec((seq, width), lambda b: (b, 0))
    tr = lambda height: pl.BlockSpec((1, height, seq), lambda b: (b, 0, 0))
    return pl.pallas_call(
        functools.partial(_block_attn_kernel, tk=seq, lam_init=lam_init),
        grid=(nb,),
        in_specs=[row(MLA_H * HEAD_PAD), row(MLA_H * HEAD_PAD), tr(MLA_H * MLA_V),
                  row(DIFF_COLS), row(DIFF_COLS), tr(DIFF_COLS),
                  _layer_spec((4, DIFF_DK), layer), _layer_spec((1, DIFF_DV), layer),
                  row(ZB_DG + DIFF_COLS)],
        out_specs=[row(MLA_H * MLA_V), row(DIFF_COLS)],
        out_shape=[jax.ShapeDtypeStruct((t, MLA_H * MLA_V), BF16), jax.ShapeDtypeStruct((t, DIFF_COLS), BF16)],
        compiler_params=_params("arbitrary"),
        name="block_attn",
    )(q, k, vt, qd, kd, vdt, lw['lam_p'], lw['diff_norm'], zb)


def _merge_kernel(yr_ref, ym_ref, yd_ref, x_ref, mod_ref, gpre_ref, wg_ref, wr_ref, wm_ref, wd_ref, wo_ref,
                  gpost_ref, o_ref):
    x = x_ref[...]
    h = _modulated_norm(x, gpre_ref, mod_ref).astype(BF16)
    merged = None
    for k, (y_ref, w_ref) in enumerate(((yr_ref, wr_ref), (ym_ref, wm_ref), (yd_ref, wd_ref))):
        gate = _sigmoid(jnp.dot(h, wg_ref[:, k * D_MODEL:(k + 1) * D_MODEL], preferred_element_type=F32))
        term = gate * jnp.dot(y_ref[...], w_ref[...], preferred_element_type=F32)
        merged = term if merged is None else merged + term
    o = jnp.dot(merged.astype(BF16), wo_ref[...], preferred_element_type=F32)
    o_ref[...] = x + mod_ref[0, 2:3, :] * (_rms(o) * gpost_ref[...])


def _merge(y_rnn, y_mla, y_diff, x, mod, lw, layer, rows_per_mod):
    t, d = x.shape
    tm = 512
    tiles_per_mod = rows_per_mod // tm
    row = lambda width: pl.BlockSpec((tm, width), lambda i: (i, 0))
    return pl.pallas_call(
        _merge_kernel,
        grid=(t // tm,),
        in_specs=[row(D_RNN), row(MLA_H * MLA_V), row(DIFF_COLS), row(d),
                  pl.BlockSpec((1, 3, d), lambda i: (i // tiles_per_mod, 0, 0)),
                  _layer_spec((1, d), layer), _layer_spec((d, 3 * d), layer),
                  _layer_spec((D_RNN, d), layer), _layer_spec((MLA_H * MLA_V, d), layer),
                  _layer_spec((DIFF_COLS, d), layer), _layer_spec((d, d), layer), _layer_spec((1, d), layer)],
        out_specs=row(d),
        out_shape=jax.ShapeDtypeStruct((t, d), F32),
        compiler_params=_params("arbitrary"),
        name="merge_out",
    )(y_rnn, y_mla, y_diff, x, mod, lw['g_pre'], lw['w_mgate'], lw['w_br_rnn'], lw['w_br_mla'], lw['w_br_diff'],
      lw['w_out'], lw['g_post'])


def _pad_heads(w, width):
    l, k, _ = w.shape
    w4 = jnp.pad(w.reshape(l, k, MLA_H, width), ((0, 0), (0, 0), (0, 0), (0, HEAD_PAD - width)))
    return w4.reshape(l, k, MLA_H * HEAD_PAD)


def _layer_params(p):
    depth, d, _ = p['w_in'].shape
    w_in = p['w_in']
    kr_pad = ((0, 0), (0, 0), (MLA_NOPE, HEAD_PAD - MLA_NOPE - MLA_ROPE))
    eye = jnp.eye(RNN_BLOCKS, dtype=F32)[None, None, :, None, :, None]
    dense = lambda w: (w[:, :, :, :, None, :] * eye).reshape(depth, 2, D_RNN, D_RNN)
    ukv = p['w_ukv'].reshape(depth, KV_RANK, MLA_H, MLA_NOPE + MLA_V)
    return dict(
        g_pre=p['g_pre'].reshape(depth, 1, d), g_post=p['g_post'].reshape(depth, 1, d),
        w_a=w_in[:, :, :_O_KR].astype(BF16), w_krp=jnp.pad(w_in[:, :, _O_KR:_O_MG], kr_pad).astype(BF16),
        w_b=w_in[:, :, _O_MG:_O_MGATE].astype(BF16), w_mgate=w_in[:, :, _O_MGATE:].astype(BF16),
        conv_w=p['conv_w'], conv_b=p['conv_b'].reshape(depth, 1, D_RNN),
        wg=(0.5 * jnp.concatenate([dense(p['w_rg_a']), dense(p['w_rg_x'])], axis=-1)).astype(BF16),
        bg=0.5 * jnp.concatenate([p['b_rg_a'], p['b_rg_x']], axis=-1).reshape(depth, 2, 1, 2 * D_RNN),
        lam=p['rg_lam'].reshape(depth, 2, 1, D_RNN),
        q_norm=p['q_norm'].reshape(depth, 1, Q_RANK), kv_norm=p['kv_norm'].reshape(depth, 1, KV_RANK),
        wuq=_pad_heads(p['w_uq'], MLA_NOPE + MLA_ROPE).astype(BF16),
        wuk=_pad_heads(ukv[..., :MLA_NOPE].reshape(depth, KV_RANK, MLA_H * MLA_NOPE), MLA_NOPE).astype(BF16),
        wuvt=jnp.swapaxes(ukv[..., MLA_NOPE:].reshape(depth, KV_RANK, MLA_H * MLA_V), 1, 2).astype(BF16),
        lam_p=jnp.stack([p['lam_q1'], p['lam_k1'], p['lam_q2'], p['lam_k2']], axis=1),
        diff_norm=p['diff_norm'].reshape(depth, 1, DIFF_DV),
        w_br_rnn=p['w_br_rnn'].astype(BF16), w_br_mla=p['w_br_mla'].astype(BF16),
        w_br_diff=p['w_br_diff'].astype(BF16), w_out=p['w_out'].astype(BF16))


def _rope_tables(n_pos, groups, dim):
    half = dim // 4
    inv = ROPE_THETA ** (-np.arange(half, dtype=np.float32) / half)
    inv_row = np.zeros((LANES,), np.float32)
    inv_col = np.zeros((LANES,), np.float32)
    first = np.zeros((LANES,), np.float32)
    second = np.zeros((LANES,), np.float32)
    for off in groups:
        for inv_axis, base in ((inv_row, off), (inv_col, off + 2 * half)):
            inv_axis[base:base + half] = inv
            inv_axis[base + half:base + 2 * half] = inv
            first[base:base + half] = 1.0
            second[base + half:base + 2 * half] = 1.0
    pos = jnp.arange(n_pos, dtype=jnp.int32)
    ang = ((pos // GRID_W).astype(F32)[:, None] * jnp.asarray(inv_row)[None, :]
           + (pos % GRID_W).astype(F32)[:, None] * jnp.asarray(inv_col)[None, :])
    sin = jnp.sin(ang)
    return jnp.stack([jnp.cos(ang), -sin * jnp.asarray(first), sin * jnp.asarray(second)])


def _layer(l, x, mod, lw, nb, seq, rows_per_mod, h0, tabs, ctx, prev_caches=()):
    tk = min(TK, seq)
    zs, q, ckvn, qd, (k, vt, kd, vdt) = _inproj(x, mod, tabs, lw, l, seq, rows_per_mod, tk, ctx is None,
                                                 prev_caches)
    y_rnn, st = _rnn(zs[0], h0, lw, l, nb, seq)
    mla_kv, diff_kv = [(k, vt)], [(kd, vdt)]
    if ctx is not None:
        kc, vtc, kdc, vdtc = _ctx(*ctx, lw, l, tk)
        mla_kv, diff_kv = [(kc, vtc)] + mla_kv, [(kdc, vdtc)] + diff_kv
    if ctx is None and seq == TQ:
        y_mla, y_diff = _block_attn(q, k, vt, qd, kd, vdt, zs[1], lw, l, nb, seq)
    else:
        y_mla = _mla_attn(q, mla_kv, zs[1], nb, seq, tk)
        y_diff = _diff_attn(qd, diff_kv, zs[1], lw, l, nb, seq, tk)
    x_new = _merge(y_rnn, y_mla, y_diff, x, mod, lw, l, rows_per_mod)
    return x_new, zs[2:], ckvn, st


def kernel(x_prompt, x_sample, cache_mla_ckv, cache_mla_krope, cache_diff_k, cache_diff_v, state_rnn, c, c_ctx, w_mod, b_mod, g_pre, g_post, w_in, conv_w, conv_b, w_rg_a, b_rg_a, w_rg_x, b_rg_x, rg_lam, q_norm, w_uq, kv_norm, w_ukv, lam_q1, lam_k1, lam_q2, lam_k2, diff_norm, w_br_rnn, w_br_mla, w_br_diff, w_out):
    depth = w_in.shape[0]
    nbp, sp, d = x_prompt.shape
    nbs, ss, _ = x_sample.shape

    n_cond = -(-(1 + nbs) // SUBLANES) * SUBLANES
    cond = jnp.concatenate([c_ctx[None, :], c, jnp.zeros((n_cond - 1 - nbs, d), F32)], axis=0)
    mod = _modulation(cond, w_mod, b_mod).reshape(depth, n_cond, 3, d)

    lw = _layer_params(dict(
        w_in=w_in, g_pre=g_pre, g_post=g_post, conv_w=conv_w, conv_b=conv_b, w_rg_a=w_rg_a, b_rg_a=b_rg_a,
        w_rg_x=w_rg_x, b_rg_x=b_rg_x, rg_lam=rg_lam, q_norm=q_norm, w_uq=w_uq, kv_norm=kv_norm, w_ukv=w_ukv,
        lam_q1=lam_q1, lam_k1=lam_k1, lam_q2=lam_q2, lam_k2=lam_k2, diff_norm=diff_norm, w_br_rnn=w_br_rnn,
        w_br_mla=w_br_mla, w_br_diff=w_br_diff, w_out=w_out))

    xp = x_prompt.reshape(nbp * sp, d)
    h0 = jnp.zeros((nbp, 2, D_RNN), F32)
    caches, st_l = [], []
    for l in range(depth):
        xp, (krc, dkc, dvc), ckvn, st = _layer(l, xp, mod[l, 0:1], lw, nbp, sp, nbp * sp, h0, None, None,
                                               caches if l == depth - 1 else ())
        caches.append((krc, dkc, dvc, ckvn))
        st_l.append(st)
    new_kr, new_dk, new_dv, new_ckv = caches[-1]

    tabs = (_rope_tables(ss, (MLA_NOPE,), MLA_ROPE), _rope_tables(ss, (0, DIFF_DK), DIFF_DK))
    ctx = (cache_mla_ckv, cache_mla_krope, cache_diff_k, cache_diff_v)
    xs = x_sample.reshape(nbs * ss, d)
    for l in range(depth):
        xs, _, _, _ = _layer(l, xs, mod[l, 1:1 + nbs], lw, nbs, ss, ss, state_rnn[:, l], tabs, ctx)

    return (xp.reshape(nbp, sp, d), xs.reshape(nbs, ss, d), new_ckv, new_kr, new_dk, new_dv, jnp.stack(st_l, axis=1))
```
